```python
import jax, jax.numpy as jnp
from jax import lax
import numpy as np

D_MODEL = 1024
BATCH = 2
SEQ = 8192
DEPTH = 2
DEC_BATCH = 128
DEC_SEQ = 4
PAST_LEN = 2048
PAGE_SIZE = 128

N_HEADS = 8
HEAD_DIM = 64
ATT_WIDTH = N_HEADS * HEAD_DIM
SG_GROUPS = 8
SG_GROUP_DIM = 64
SG_WIDTH = SG_GROUPS * SG_GROUP_DIM
CHUNK = 128
D_FF = 4 * D_MODEL
Q_BLOCK = 128
EPS = 1e-6
IN_WIDTH = 3 * ATT_WIDTH + 2 * SG_WIDTH + 2 * D_MODEL

kernel_name = "hybrid_gmlp_stickbreaking_decoder_step"


def rmsnorm(x, g):
    x32 = x.astype(jnp.float32)
    y = x32 * lax.rsqrt(jnp.mean(x32 * x32, axis=-1, keepdims=True) + EPS) * g.astype(jnp.float32)
    return y.astype(x.dtype)


def in_projection(xn, w_in, b_gate, g_sv):
    b, t = xn.shape[:2]
    h = jnp.einsum('btd,de->bte', xn, w_in)
    offs = [ATT_WIDTH, 2 * ATT_WIDTH, 3 * ATT_WIDTH, 3 * ATT_WIDTH + SG_WIDTH, 3 * ATT_WIDTH + 2 * SG_WIDTH]
    q, k, v, u, sv, gates = jnp.split(h, offs, axis=-1)
    heads = lambda a: a.reshape(b, t, N_HEADS, HEAD_DIM)
    u = jax.nn.gelu(u)
    sv = rmsnorm(jax.nn.gelu(sv), g_sv).reshape(b, t, SG_GROUPS, SG_GROUP_DIM)
    gates = jax.nn.sigmoid(gates + b_gate)
    g_att, g_sg = jnp.split(gates, 2, axis=-1)
    return heads(q), heads(k), heads(v), u, sv, g_att, g_sg


def stick_breaking_mix(q, k, v, qpos, kpos, b_sb):
    z = jnp.einsum('bqhd,bshd->bhqs', q, k).astype(jnp.float32) * (HEAD_DIM ** -0.5)
    z = z + b_sb.astype(jnp.float32)[None, :, None, None]
    mask = kpos[None, :] < qpos[:, None]
    log_1mb = jnp.where(mask, jax.nn.log_sigmoid(-z), 0.0)
    suffix = lax.cumsum(log_1mb, axis=3, reverse=True) - log_1mb
    a = jnp.where(mask, jnp.exp(jax.nn.log_sigmoid(z) + suffix), 0.0)
    return jnp.einsum('bhqs,bshd->bqhd', a.astype(v.dtype), v)


def stick_breaking_prompt(q, k, v, b_sb):
    b, s = q.shape[:2]
    nb = s // Q_BLOCK
    qb = q.reshape(b, nb, Q_BLOCK, N_HEADS, HEAD_DIM).swapaxes(0, 1)
    starts = jnp.arange(nb, dtype=jnp.int32) * Q_BLOCK
    kpos = jnp.arange(s, dtype=jnp.int32)

    def block(args):
        qblk, st = args
        return stick_breaking_mix(qblk, k, v, st + jnp.arange(Q_BLOCK, dtype=jnp.int32), kpos, b_sb)

    out = lax.map(block, (qb, starts))
    return out.swapaxes(0, 1).reshape(b, s, ATT_WIDTH)


def stick_breaking_sample(q, k_new, v_new, cache_k_l, cache_v_l, page_table, b_sb):
    db, t = q.shape[:2]
    k_past = cache_k_l[page_table].reshape(db, -1, N_HEADS, HEAD_DIM).astype(k_new.dtype)
    v_past = cache_v_l[page_table].reshape(db, -1, N_HEADS, HEAD_DIM).astype(v_new.dtype)
    p = k_past.shape[1]
    k_all = jnp.concatenate([k_past, k_new], axis=1)
    v_all = jnp.concatenate([v_past, v_new], axis=1)
    kpos = jnp.arange(p + t, dtype=jnp.int32)
    qpos = p + jnp.arange(t, dtype=jnp.int32)
    return stick_breaking_mix(q, k_all, v_all, qpos, kpos, b_sb).reshape(db, t, ATT_WIDTH)


def spatial_gating(u, sv, w_sg, b_sg):
    b, t = u.shape[:2]
    L = min(t, CHUNK)
    nc = t // L
    tril = jnp.tril(jnp.ones((CHUNK, CHUNK), dtype=bool))
    w = jnp.where(tril[None], w_sg, 0.0)[:, :L, :L]
    svc = sv.reshape(b, nc, L, SG_GROUPS, SG_GROUP_DIM)
    mixed = jnp.einsum('gts,bnsgc->bntgc', w, svc) + b_sg[:, :L].T[:, :, None]
    return u * mixed.reshape(b, t, SG_WIDTH)


def merge_branches(att, sg, g_att, g_sg, w_o_att, w_o_sg, w_o):
    merged = g_att * jnp.einsum('bte,ed->btd', att, w_o_att) + g_sg * jnp.einsum('bte,ed->btd', sg, w_o_sg)
    return jnp.einsum('btd,de->bte', merged, w_o)


def squared_relu_mlp(xn, w1, w2):
    h = jax.nn.relu(jnp.einsum('btd,df->btf', xn, w1))
    return jnp.einsum('btf,fd->btd', h * h, w2)


def setup_inputs(seed: int = 0) -> dict:
    key = jax.random.key(seed)
    ks = jax.random.split(key, 20)
    n_pages = PAST_LEN // PAGE_SIZE
    n_used = DEC_BATCH * n_pages
    n_pool = (n_used * 5) // 4
    nrm = jax.random.normal
    f32 = jnp.float32
    x_prompt = nrm(ks[0], (BATCH, SEQ, D_MODEL), f32)
    x_sample = nrm(ks[1], (DEC_BATCH, DEC_SEQ, D_MODEL), f32)
    cache_k = nrm(ks[2], (DEPTH, n_pool, PAGE_SIZE, N_HEADS, HEAD_DIM), f32)
    cache_v = nrm(ks[3], (DEPTH, n_pool, PAGE_SIZE, N_HEADS, HEAD_DIM), f32)
    page_table = jax.random.permutation(ks[4], n_pool)[:n_used].reshape(DEC_BATCH, n_pages).astype(jnp.int32)
    g_mix = 1.0 + 0.05 * nrm(ks[5], (DEPTH, D_MODEL), f32)
    w_in = nrm(ks[6], (DEPTH, D_MODEL, IN_WIDTH), f32) * D_MODEL ** -0.5
    b_gate = 0.05 * nrm(ks[7], (DEPTH, 2 * D_MODEL), f32)
    b_sb = -8.0 - 1.0 * jax.random.uniform(ks[18], (DEPTH, N_HEADS), f32)
    g_sv = 1.0 + 0.05 * nrm(ks[8], (DEPTH, SG_WIDTH), f32)
    w_sg = nrm(ks[9], (DEPTH, SG_GROUPS, CHUNK, CHUNK), f32) * CHUNK ** -0.5
    b_sg = 1.0 + 0.1 * nrm(ks[10], (DEPTH, SG_GROUPS, CHUNK), f32)
    w_o_att = nrm(ks[11], (DEPTH, ATT_WIDTH, D_MODEL), f32) * ATT_WIDTH ** -0.5
    w_o_sg = nrm(ks[12], (DEPTH, SG_WIDTH, D_MODEL), f32) * SG_WIDTH ** -0.5
    w_o = nrm(ks[13], (DEPTH, D_MODEL, D_MODEL), f32) * D_MODEL ** -0.5
    g_ffn = 1.0 + 0.05 * nrm(ks[14], (DEPTH, D_MODEL), f32)
    w_ff1 = nrm(ks[15], (DEPTH, D_MODEL, D_FF), f32) * D_MODEL ** -0.5
    w_ff2 = nrm(ks[16], (DEPTH, D_FF, D_MODEL), f32) * D_FF ** -0.5
    g_final = 1.0 + 0.05 * nrm(ks[17], (D_MODEL,), f32)
    return {"x_prompt": x_prompt, "x_sample": x_sample, "cache_k": cache_k, "cache_v": cache_v,
            "page_table": page_table, "g_mix": g_mix, "w_in": w_in, "b_gate": b_gate, "b_sb": b_sb,
            "g_sv": g_sv, "w_sg": w_sg, "b_sg": b_sg, "w_o_att": w_o_att, "w_o_sg": w_o_sg, "w_o": w_o,
            "g_ffn": g_ffn, "w_ff1": w_ff1, "w_ff2": w_ff2, "g_final": g_final}


def reference(x_prompt, x_sample, cache_k, cache_v, page_table, g_mix, w_in, b_gate, b_sb, g_sv, w_sg, b_sg,
              w_o_att, w_o_sg, w_o, g_ffn, w_ff1, w_ff2, g_final):
    xp, xs = x_prompt, x_sample
    kp_l, vp_l, ks_l, vs_l, sgv_l = [], [], [], [], []
    for l in range(DEPTH):
        q, k, v, u, sv, ga, gs = in_projection(rmsnorm(xp, g_mix[l]), w_in[l], b_gate[l], g_sv[l])
        att = stick_breaking_prompt(q, k, v, b_sb[l])
        sg = spatial_gating(u, sv, w_sg[l], b_sg[l])
        xp = xp + merge_branches(att, sg, ga, gs, w_o_att[l], w_o_sg[l], w_o[l])
        kp_l.append(k)
        vp_l.append(v)
        q, k, v, u, sv, ga, gs = in_projection(rmsnorm(xs, g_mix[l]), w_in[l], b_gate[l], g_sv[l])
        att = stick_breaking_sample(q, k, v, cache_k[l], cache_v[l], page_table, b_sb[l])
        sg = spatial_gating(u, sv, w_sg[l], b_sg[l])
        xs = xs + merge_branches(att, sg, ga, gs, w_o_att[l], w_o_sg[l], w_o[l])
        ks_l.append(k)
        vs_l.append(v)
        sgv_l.append(sv)
        xp = xp + squared_relu_mlp(rmsnorm(xp, g_ffn[l]), w_ff1[l], w_ff2[l])
        xs = xs + squared_relu_mlp(rmsnorm(xs, g_ffn[l]), w_ff1[l], w_ff2[l])
    y_prompt = rmsnorm(xp, g_final)
    y_sample = rmsnorm(xs, g_final)
    new_k_prompt = jnp.stack(kp_l)
    new_v_prompt = jnp.stack(vp_l)
    new_k_sample = jnp.stack(ks_l)
    new_v_sample = jnp.stack(vs_l)
    new_sgv_sample = jnp.stack(sgv_l)
    return (y_prompt, y_sample, new_k_prompt, new_v_prompt, new_k_sample, new_v_sample, new_sgv_sample)
```

```python
import functools
import math

import jax
import jax.numpy as jnp
from jax import lax
from jax.experimental import pallas as pl
from jax.experimental.pallas import tpu as pltpu

F32 = jnp.float32
BF16 = jnp.bfloat16

D_MODEL = 1024
N_HEADS = 8
HEAD_DIM = 64
ATT_WIDTH = N_HEADS * HEAD_DIM
SG_GROUPS = 8
SG_GROUP_DIM = 64
SG_WIDTH = SG_GROUPS * SG_GROUP_DIM
CHUNK = 128
D_FF = 4 * D_MODEL
PAGE_SIZE = 128
EPS = 1e-6
IN_WIDTH = 3 * ATT_WIDTH + 2 * SG_WIDTH + 2 * D_MODEL

LANES = 128
SUBLANES = 8
LOG2E = math.log2(math.e)
Q_SCALE = -(HEAD_DIM ** -0.5) * LOG2E

TOKEN_TILE = 512
ATT_BQ = 512
ATT_BK = 128
FF_CHUNK = 1024
VMEM_LIMIT = 56 * 1024 * 1024


def _rms(x, g):
    return x * lax.rsqrt(jnp.mean(x * x, axis=-1, keepdims=True) + EPS) * g


def _const_spec(shape):
    return pl.BlockSpec(shape, lambda *_: (0,) * len(shape), pipeline_mode=pl.Buffered(1))


def _log2_one_minus_beta(nz):
    return jnp.minimum(nz, 0.0) - jnp.log2(1.0 + jnp.exp2(-jnp.abs(nz)))


def _inproj_kernel(x_ref, g_ref, w_ref, bg_ref, gsv_ref,
                   q_ref, k_ref, v_ref, kb_ref, vb_ref, u_ref, sv_ref, gate_ref):
    xn = _rms(x_ref[...], g_ref[...]).astype(BF16)

    def proj(lo, hi):
        return jnp.dot(xn, w_ref[:, lo:hi], preferred_element_type=F32)

    o_k, o_v, o_u, o_sv, o_g = ATT_WIDTH, 2 * ATT_WIDTH, 3 * ATT_WIDTH, 3 * ATT_WIDTH + SG_WIDTH, 3 * ATT_WIDTH + 2 * SG_WIDTH
    q_ref[...] = (proj(0, o_k) * Q_SCALE).astype(BF16)
    k = proj(o_k, o_v)
    k_ref[...] = k
    kb_ref[...] = k.astype(BF16)
    v = proj(o_v, o_u)
    v_ref[...] = v
    vb_ref[...] = v.astype(BF16)
    u_ref[...] = jax.nn.gelu(proj(o_u, o_sv))
    sv_ref[...] = _rms(jax.nn.gelu(proj(o_sv, o_g)), gsv_ref[...])
    gate_ref[...] = jax.nn.sigmoid(proj(o_g, IN_WIDTH) + bg_ref[...])


def _in_projection(x, g, w_bf, b_gate, g_sv):
    n = x.shape[0]
    tm = min(TOKEN_TILE, n)
    row = lambda w: pl.BlockSpec((tm, w), lambda i: (i, 0))
    return pl.pallas_call(
        _inproj_kernel,
        grid=(n // tm,),
        in_specs=[row(D_MODEL), _const_spec((1, D_MODEL)), _const_spec((D_MODEL, IN_WIDTH)),
                  _const_spec((1, 2 * D_MODEL)), _const_spec((1, SG_WIDTH))],
        out_specs=[row(ATT_WIDTH)] * 5 + [row(SG_WIDTH)] * 2 + [row(2 * D_MODEL)],
        out_shape=[jax.ShapeDtypeStruct((n, ATT_WIDTH), BF16),
                   jax.ShapeDtypeStruct((n, ATT_WIDTH), F32),
                   jax.ShapeDtypeStruct((n, ATT_WIDTH), F32),
                   jax.ShapeDtypeStruct((n, ATT_WIDTH), BF16),
                   jax.ShapeDtypeStruct((n, ATT_WIDTH), BF16),
                   jax.ShapeDtypeStruct((n, SG_WIDTH), F32),
                   jax.ShapeDtypeStruct((n, SG_WIDTH), F32),
                   jax.ShapeDtypeStruct((n, 2 * D_MODEL), F32)],
        compiler_params=pltpu.CompilerParams(dimension_semantics=("arbitrary",), vmem_limit_bytes=VMEM_LIMIT),
        name="in_projection",
    )(x, g.reshape(1, -1), w_bf, b_gate.reshape(1, -1), g_sv.reshape(1, -1))


def _attn_prompt_kernel(q_ref, k_ref, v_ref, bias_ref, o_ref, kst, vst, u2, o_acc, c_acc, *, bq, bk, seq):
    i = pl.program_id(2)
    nkb = seq // bk
    per_tile = bq // bk

    @pl.when(i == 0)
    def _():
        lane = lax.broadcasted_iota(jnp.int32, (1, LANES), 1)
        even = jnp.where(lane < HEAD_DIM, 1.0, 0.0).astype(BF16)
        odd = jnp.where(lane < HEAD_DIM, 0.0, 1.0).astype(BF16)
        k3 = k_ref[...].reshape(nkb, bk, LANES)
        kst[:, 0:bk, :] = k3 * even
        kst[:, bk:2 * bk, :] = k3 * odd
        v3 = v_ref[...].reshape(nkb, bk, LANES)
        vst[:, 0:bk, :] = v3 * even
        vst[:, bk:2 * bk, :] = v3 * odd
        r = lax.broadcasted_iota(jnp.int32, (2 * bk, 2 * bk), 0)
        c = lax.broadcasted_iota(jnp.int32, (2 * bk, 2 * bk), 1)
        same_head = (r < bk) == (c < bk)
        u2[...] = jnp.where(jnp.logical_and(r > c, same_head), 1.0, 0.0).astype(BF16)

    q = q_ref[...]
    bias = bias_ref[...]
    o_acc[...] = jnp.zeros_like(o_acc)
    c_acc[...] = jnp.zeros_like(c_acc)

    def tile(j, mask):
        nz = lax.dot_general(q, kst[j], (((1,), (1,)), ((), ())), preferred_element_type=F32) + bias
        lsn = _log2_one_minus_beta(nz)
        if mask is not None:
            lsn = jnp.where(mask, lsn, 0.0)
        suffix = jnp.dot(lsn.astype(BF16), u2[...], preferred_element_type=F32)
        carry = c_acc[...]
        a = jnp.exp2(lsn - nz + suffix + carry)
        if mask is not None:
            a = jnp.where(mask, a, 0.0)
        o_acc[...] += jnp.dot(a.astype(BF16), vst[j], preferred_element_type=F32)
        t_even = jnp.sum(lsn[:, :bk], axis=1, keepdims=True)
        t_odd = jnp.sum(lsn[:, bk:], axis=1, keepdims=True)
        c_acc[...] = carry + jnp.concatenate(
            [jnp.broadcast_to(t_even, (bq, bk)), jnp.broadcast_to(t_odd, (bq, bk))], axis=1)

    row = lax.broadcasted_iota(jnp.int32, (bq, 2 * bk), 0)
    col = lax.broadcasted_iota(jnp.int32, (bq, 2 * bk), 1) & (bk - 1)
    for d in reversed(range(per_tile)):
        tile(i * per_tile + d, col + d * bk < row)

    n_full = i * per_tile

    def body(t, _):
        tile(n_full - 1 - t, None)
        return 0

    lax.fori_loop(0, n_full, body, 0)
    o_ref[...] = o_acc[...].astype(o_ref.dtype)


def _attention_prompt(q, kb, vb, bias2):
    b, s, _ = q.shape
    bq, bk = ATT_BQ, ATT_BK
    n_pairs = N_HEADS // 2
    bias_cat = jnp.repeat(bias2, bk).reshape(n_pairs, 1, 2 * bk)
    kern = functools.partial(_attn_prompt_kernel, bq=bq, bk=bk, seq=s)
    return pl.pallas_call(
        kern,
        grid=(b, n_pairs, s // bq),
        in_specs=[pl.BlockSpec((None, bq, LANES), lambda b_, p, i: (b_, i, p)),
                  pl.BlockSpec((None, s, LANES), lambda b_, p, i: (b_, 0, p)),
                  pl.BlockSpec((None, s, LANES), lambda b_, p, i: (b_, 0, p)),
                  pl.BlockSpec((None, 1, 2 * bk), lambda b_, p, i: (p, 0, 0))],
        out_specs=pl.BlockSpec((None, bq, LANES), lambda b_, p, i: (b_, i, p)),
        out_shape=jax.ShapeDtypeStruct((b, s, ATT_WIDTH), BF16),
        scratch_shapes=[pltpu.VMEM((s // bk, 2 * bk, LANES), BF16),
                        pltpu.VMEM((s // bk, 2 * bk, LANES), BF16),
                        pltpu.VMEM((2 * bk, 2 * bk), BF16),
                        pltpu.VMEM((bq, LANES), F32),
                        pltpu.VMEM((bq, 2 * bk), F32)],
        compiler_params=pltpu.CompilerParams(dimension_semantics=("arbitrary",) * 3, vmem_limit_bytes=VMEM_LIMIT),
        name="attention_prompt",
    )(q, kb, vb, bias_cat)


def _attn_sample_kernel(pt_ref, q_ref, kn_ref, vn_ref, bias_ref, *rest, n_pages):
    del pt_ref
    k_refs = rest[:n_pages]
    v_refs = rest[n_pages:2 * n_pages]
    o_ref = rest[2 * n_pages]
    kn_scr, vn_scr, u_scr = rest[2 * n_pages + 1:]
    rows = N_HEADS * SUBLANES

    @pl.when(pl.program_id(0) == 0)
    def _():
        kn_scr[...] = jnp.zeros_like(kn_scr)
        vn_scr[...] = jnp.zeros_like(vn_scr)
        r = lax.broadcasted_iota(jnp.int32, (PAGE_SIZE, PAGE_SIZE), 0)
        c = lax.broadcasted_iota(jnp.int32, (PAGE_SIZE, PAGE_SIZE), 1)
        u_scr[...] = jnp.where(r > c, 1.0, 0.0).astype(BF16)

    kn_scr[:, 0:SUBLANES, :] = kn_ref[...]
    vn_scr[:, 0:SUBLANES, :] = vn_ref[...]
    q = q_ref[...]
    bias = bias_ref[...]

    def by_head(ref):
        return jnp.stack([ref[pl.ds(h, PAGE_SIZE, stride=N_HEADS), :] for h in range(N_HEADS)]).astype(BF16)

    def block(kh, vh, carry, acc, mask):
        nz = jnp.einsum("htd,hsd->hts", q, kh, preferred_element_type=F32).reshape(rows, PAGE_SIZE) + bias
        lsn = _log2_one_minus_beta(nz)
        if mask is not None:
            lsn = jnp.where(mask, lsn, 0.0)
        suffix = jnp.dot(lsn.astype(BF16), u_scr[...], preferred_element_type=F32)
        a = jnp.exp2(lsn - nz + suffix + carry)
        if mask is not None:
            a = jnp.where(mask, a, 0.0)
        a3 = a.reshape(N_HEADS, SUBLANES, PAGE_SIZE).astype(BF16)
        acc = acc + jnp.einsum("hts,hsd->htd", a3, vh, preferred_element_type=F32)
        return carry + jnp.sum(lsn, axis=1, keepdims=True), acc

    t_new = lax.broadcasted_iota(jnp.int32, (rows, PAGE_SIZE), 0) & (SUBLANES - 1)
    j_new = lax.broadcasted_iota(jnp.int32, (rows, PAGE_SIZE), 1)
    carry = jnp.zeros((rows, 1), F32)
    acc = jnp.zeros((N_HEADS, SUBLANES, HEAD_DIM), F32)
    carry, acc = block(kn_scr[...].astype(BF16), vn_scr[...].astype(BF16), carry, acc, j_new < t_new)
    for p in reversed(range(n_pages)):
        carry, acc = block(by_head(k_refs[p]), by_head(v_refs[p]), carry, acc, None)
    o_ref[...] = acc


def _attention_sample(q, k_new, v_new, cache_k, cache_v, layer, page_table, bias2):
    db, t, _ = q.shape
    n_pages = page_table.shape[1]
    rows = N_HEADS * SUBLANES

    def heads_first(a):
        a = a.reshape(db, t, N_HEADS, HEAD_DIM).transpose(0, 2, 1, 3)
        return jnp.pad(a, ((0, 0), (0, 0), (0, SUBLANES - t), (0, 0)))

    bias_rows = jnp.broadcast_to(jnp.repeat(bias2, SUBLANES)[:, None], (rows, PAGE_SIZE))
    small = lambda: pl.BlockSpec((None, N_HEADS, SUBLANES, HEAD_DIM), lambda b, pt: (b, 0, 0, 0))
    page = lambda i: pl.BlockSpec((None, None, PAGE_SIZE * N_HEADS, HEAD_DIM),
                                  lambda b, pt, i=i: (layer, pt[b, i], 0, 0))
    grid_spec = pltpu.PrefetchScalarGridSpec(
        num_scalar_prefetch=1,
        grid=(db,),
        in_specs=[small(), small(), small(), pl.BlockSpec((rows, PAGE_SIZE), lambda b, pt: (0, 0))]
                 + [page(i) for i in range(n_pages)] * 2,
        out_specs=small(),
        scratch_shapes=[pltpu.VMEM((N_HEADS, PAGE_SIZE, HEAD_DIM), F32),
                        pltpu.VMEM((N_HEADS, PAGE_SIZE, HEAD_DIM), F32),
                        pltpu.VMEM((PAGE_SIZE, PAGE_SIZE), BF16)],
    )
    out = pl.pallas_call(
        functools.partial(_attn_sample_kernel, n_pages=n_pages),
        grid_spec=grid_spec,
        out_shape=jax.ShapeDtypeStruct((db, N_HEADS, SUBLANES, HEAD_DIM), F32),
        compiler_params=pltpu.CompilerParams(dimension_semantics=("arbitrary",), vmem_limit_bytes=VMEM_LIMIT),
        name="attention_sample",
    )(page_table, heads_first(q), heads_first(k_new), heads_first(v_new), bias_rows,
      *([cache_k] * n_pages), *([cache_v] * n_pages))
    return out[:, :, :t, :].transpose(0, 2, 1, 3).reshape(db * t, ATT_WIDTH).astype(BF16)


def _merge_kernel(att_ref, u_ref, sv_ref, gate_ref, x_ref, woa_ref, wos_ref, wo_ref, wmix_ref, bmix_ref,
                  o_ref, mix_scr, *, chunk_len, tm):
    @pl.when(pl.program_id(0) == 0)
    def _():
        shift = chunk_len.bit_length() - 1
        r = lax.broadcasted_iota(jnp.int32, (CHUNK, CHUNK), 0)
        c = lax.broadcasted_iota(jnp.int32, (CHUNK, CHUNK), 1)
        keep = jnp.logical_and((r >> shift) == (c >> shift), c <= r)
        mix_scr[...] = jnp.where(keep[None], wmix_ref[...], 0.0).astype(BF16)

    first_group = lax.broadcasted_iota(jnp.int32, (1, LANES), 1) < SG_GROUP_DIM
    pieces = []
    for ch in range(tm // CHUNK):
        rows = slice(ch * CHUNK, (ch + 1) * CHUNK)
        svc = sv_ref[rows, :].astype(BF16)
        cols = []
        for p in range(SG_WIDTH // LANES):
            blk = svc[:, p * LANES:(p + 1) * LANES]
            m0 = jnp.dot(mix_scr[2 * p], blk, preferred_element_type=F32)
            m1 = jnp.dot(mix_scr[2 * p + 1], blk, preferred_element_type=F32)
            cols.append(jnp.where(first_group, m0, m1))
        mixed = jnp.concatenate(cols, axis=1) + bmix_ref[...]
        pieces.append(u_ref[rows, :] * mixed)
    sg = jnp.concatenate(pieces, axis=0).astype(BF16)
    merged = (gate_ref[:, :D_MODEL] * jnp.dot(att_ref[...], woa_ref[...], preferred_element_type=F32)
              + gate_ref[:, D_MODEL:] * jnp.dot(sg, wos_ref[...], preferred_element_type=F32))
    o_ref[...] = x_ref[...] + jnp.dot(merged.astype(BF16), wo_ref[...], preferred_element_type=F32)


def _merge(att, u, sv, gates, x, woa_bf, wos_bf, wo_bf, w_sg, b_sg, chunk_len):
    n = x.shape[0]
    tm = min(TOKEN_TILE, n)
    reps = CHUNK // chunk_len
    wmix = jnp.tile(w_sg[:, :chunk_len, :chunk_len], (1, reps, reps))
    bmix = jnp.repeat(jnp.tile(b_sg[:, :chunk_len].T, (reps, 1)), SG_GROUP_DIM, axis=1)
    row = lambda w: pl.BlockSpec((tm, w), lambda i: (i, 0))
    return pl.pallas_call(
        functools.partial(_merge_kernel, chunk_len=chunk_len, tm=tm),
        grid=(n // tm,),
        in_specs=[row(ATT_WIDTH), row(SG_WIDTH), row(SG_WIDTH), row(2 * D_MODEL), row(D_MODEL),
                  _const_spec((ATT_WIDTH, D_MODEL)), _const_spec((SG_WIDTH, D_MODEL)),
                  _const_spec((D_MODEL, D_MODEL)), _const_spec((SG_GROUPS, CHUNK, CHUNK)),
                  _const_spec((CHUNK, SG_WIDTH))],
        out_specs=row(D_MODEL),
        out_shape=jax.ShapeDtypeStruct((n, D_MODEL), F32),
        scratch_shapes=[pltpu.VMEM((SG_GROUPS, CHUNK, CHUNK), BF16)],
        compiler_params=pltpu.CompilerParams(dimension_semantics=("arbitrary",), vmem_limit_bytes=VMEM_LIMIT),
        name="merge",
    )(att, u, sv, gates, x, woa_bf, wos_bf, wo_bf, wmix, bmix)


def _mlp_kernel(x_ref, g_ref, w1_ref, w2_ref, gf_ref, o_ref, *, final):
    x = x_ref[...]
    xn = _rms(x, g_ref[...]).astype(BF16)
    acc = x
    for c in range(D_FF // FF_CHUNK):
        cols = slice(c * FF_CHUNK, (c + 1) * FF_CHUNK)
        h = jnp.maximum(jnp.dot(xn, w1_ref[:, cols], preferred_element_type=F32), 0.0)
        acc = acc + jnp.dot((h * h).astype(BF16), w2_ref[cols, :], preferred_element_type=F32)
    if final:
        acc = _rms(acc, gf_ref[...])
    o_ref[...] = acc


def _mlp(x, g, w1_bf, w2_bf, g_final, final):
    n = x.shape[0]
    tm = min(TOKEN_TILE, n)
    row = pl.BlockSpec((tm, D_MODEL), lambda i: (i, 0))
    return pl.pallas_call(
        functools.partial(_mlp_kernel, final=final),
        grid=(n // tm,),
        in_specs=[row, _const_spec((1, D_MODEL)), _const_spec((D_MODEL, D_FF)), _const_spec((D_FF, D_MODEL)),
                  _const_spec((1, D_MODEL))],
        out_specs=row,
        out_shape=jax.ShapeDtypeStruct((n, D_MODEL), F32),
        compiler_params=pltpu.CompilerParams(dimension_semantics=("arbitrary",), vmem_limit_bytes=VMEM_LIMIT),
        name="mlp",
    )(x, g.reshape(1, -1), w1_bf, w2_bf, g_final.reshape(1, -1))


def kernel(x_prompt, x_sample, cache_k, cache_v, page_table, g_mix, w_in, b_gate, b_sb, g_sv, w_sg, b_sg,
           w_o_att, w_o_sg, w_o, g_ffn, w_ff1, w_ff2, g_final):
    batch, seq, _ = x_prompt.shape
    dec_batch, dec_seq, _ = x_sample.shape
    depth = w_in.shape[0]
    n_pool = cache_k.shape[1]
    xp = x_prompt.reshape(batch * seq, D_MODEL)
    xs = x_sample.reshape(dec_batch * dec_seq, D_MODEL)
    ck = cache_k.reshape(depth, n_pool, PAGE_SIZE * N_HEADS, HEAD_DIM)
    cv = cache_v.reshape(depth, n_pool, PAGE_SIZE * N_HEADS, HEAD_DIM)
    kp_l, vp_l, ks_l, vs_l, sgv_l = [], [], [], [], []
    for l in range(depth):
        w_in_bf = w_in[l].astype(BF16)
        woa_bf, wos_bf, wo_bf = w_o_att[l].astype(BF16), w_o_sg[l].astype(BF16), w_o[l].astype(BF16)
        w1_bf, w2_bf = w_ff1[l].astype(BF16), w_ff2[l].astype(BF16)
        bias2 = -b_sb[l] * LOG2E
        final = l == depth - 1

        q, k, v, kb, vb, u, sv, gates = _in_projection(xp, g_mix[l], w_in_bf, b_gate[l], g_sv[l])
        att = _attention_prompt(q.reshape(batch, seq, ATT_WIDTH), kb.reshape(batch, seq, ATT_WIDTH),
                                vb.reshape(batch, seq, ATT_WIDTH), bias2).reshape(batch * seq, ATT_WIDTH)
        xp = _merge(att, u, sv, gates, xp, woa_bf, wos_bf, wo_bf, w_sg[l], b_sg[l], CHUNK)
        kp_l.append(k.reshape(batch, seq, N_HEADS, HEAD_DIM))
        vp_l.append(v.reshape(batch, seq, N_HEADS, HEAD_DIM))

        q, k, v, _, _, u, sv, gates = _in_projection(xs, g_mix[l], w_in_bf, b_gate[l], g_sv[l])
        shp = (dec_batch, dec_seq, ATT_WIDTH)
        att = _attention_sample(q.reshape(shp), k.reshape(shp), v.reshape(shp), ck, cv, l, page_table, bias2)
        xs = _merge(att, u, sv, gates, xs, woa_bf, wos_bf, wo_bf, w_sg[l], b_sg[l], dec_seq)
        ks_l.append(k.reshape(dec_batch, dec_seq, N_HEADS, HEAD_DIM))
        vs_l.append(v.reshape(dec_batch, dec_seq, N_HEADS, HEAD_DIM))
        sgv_l.append(sv.reshape(dec_batch, dec_seq, SG_GROUPS, SG_GROUP_DIM))

        xp = _mlp(xp, g_ffn[l], w1_bf, w2_bf, g_final, final)
        xs = _mlp(xs, g_ffn[l], w1_bf, w2_bf, g_final, final)

    return (xp.reshape(batch, seq, D_MODEL), xs.reshape(dec_batch, dec_seq, D_MODEL),
            jnp.stack(kp_l), jnp.stack(vp_l), jnp.stack(ks_l), jnp.stack(vs_l), jnp.stack(sgv_l))
```

```python
import functools
import math

import jax
import jax.numpy as jnp
from jax import lax
from jax.experimental import pallas as pl
from jax.experimental.pallas import tpu as pltpu

F32 = jnp.float32
BF16 = jnp.bfloat16

D_MODEL = 1024
N_HEADS = 8
HEAD_DIM = 64
ATT_WIDTH = N_HEADS * HEAD_DIM
SG_GROUPS = 8
SG_GROUP_DIM = 64
SG_WIDTH = SG_GROUPS * SG_GROUP_DIM
CHUNK = 128
D_FF = 4 * D_MODEL
PAGE_SIZE = 128
EPS = 1e-6
IN_WIDTH = 3 * ATT_WIDTH + 2 * SG_WIDTH + 2 * D_MODEL

LANES = 128
SUBLANES = 8
LOG2E = math.log2(math.e)
Q_SCALE = -(HEAD_DIM ** -0.5) * LOG2E

TOKEN_TILE = 512
ATT_BQ = 512
ATT_BK = 256
FF_CHUNK = 1024
BIAS_PIECES = 3
VMEM_LIMIT = 56 * 1024 * 1024


def _rms(x, g):
    return x * lax.rsqrt(jnp.mean(x * x, axis=-1, keepdims=True) + EPS) * g


def _const_spec(shape):
    return pl.BlockSpec(shape, lambda *_: (0,) * len(shape), pipeline_mode=pl.Buffered(1))


def _log2_one_minus_beta(nz):
    sign_bit = jnp.uint32(0x80000000)
    neg_abs = lax.bitcast_convert_type(lax.bitcast_convert_type(nz, jnp.uint32) | sign_bit, F32)
    return jnp.minimum(nz, 0.0) - jnp.log(1.0 + jnp.exp2(neg_abs)) * LOG2E


def _inproj_kernel(x_ref, g_ref, w_ref, bg_ref, gsv_ref,
                   q_ref, k_ref, v_ref, kb_ref, vb_ref, u_ref, sv_ref, gate_ref):
    xn = _rms(x_ref[...], g_ref[...]).astype(BF16)

    def proj(lo, hi):
        return jnp.dot(xn, w_ref[:, lo:hi], preferred_element_type=F32)

    o_k, o_v, o_u, o_sv, o_g = ATT_WIDTH, 2 * ATT_WIDTH, 3 * ATT_WIDTH, 3 * ATT_WIDTH + SG_WIDTH, 3 * ATT_WIDTH + 2 * SG_WIDTH
    q_ref[...] = (proj(0, o_k) * Q_SCALE).astype(BF16)
    k = proj(o_k, o_v)
    k_ref[...] = k
    kb_ref[...] = k.astype(BF16)
    v = proj(o_v, o_u)
    v_ref[...] = v
    vb_ref[...] = v.astype(BF16)
    u_ref[...] = jax.nn.gelu(proj(o_u, o_sv))
    sv_ref[...] = _rms(jax.nn.gelu(proj(o_sv, o_g)), gsv_ref[...])
    gate_ref[...] = jax.nn.sigmoid(proj(o_g, IN_WIDTH) + bg_ref[...])


def _in_projection(x, g, w_bf, b_gate, g_sv):
    n = x.shape[0]
    tm = min(TOKEN_TILE, n)
    row = lambda w: pl.BlockSpec((tm, w), lambda i: (i, 0))
    return pl.pallas_call(
        _inproj_kernel,
        grid=(n // tm,),
        in_specs=[row(D_MODEL), _const_spec((1, D_MODEL)), _const_spec((D_MODEL, IN_WIDTH)),
                  _const_spec((1, 2 * D_MODEL)), _const_spec((1, SG_WIDTH))],
        out_specs=[row(ATT_WIDTH)] * 5 + [row(SG_WIDTH)] * 2 + [row(2 * D_MODEL)],
        out_shape=[jax.ShapeDtypeStruct((n, ATT_WIDTH), BF16),
                   jax.ShapeDtypeStruct((n, ATT_WIDTH), F32),
                   jax.ShapeDtypeStruct((n, ATT_WIDTH), F32),
                   jax.ShapeDtypeStruct((n, ATT_WIDTH), BF16),
                   jax.ShapeDtypeStruct((n, ATT_WIDTH), BF16),
                   jax.ShapeDtypeStruct((n, SG_WIDTH), F32),
                   jax.ShapeDtypeStruct((n, SG_WIDTH), F32),
                   jax.ShapeDtypeStruct((n, 2 * D_MODEL), F32)],
        compiler_params=pltpu.CompilerParams(dimension_semantics=("arbitrary",), vmem_limit_bytes=VMEM_LIMIT),
        name="in_projection",
    )(x, g.reshape(1, -1), w_bf, b_gate.reshape(1, -1), g_sv.reshape(1, -1))


def _attn_prompt_kernel(q_ref, k_ref, v_ref, bias_ref, o_ref, kst, vst, u2, o_acc, c_acc,
                        nz_even, nz_odd, a_even, a_odd, *, bq, bk, seq):
    i = pl.program_id(2)
    nkb = seq // bk
    per_tile = bq // bk
    reps = bk // LANES
    nz_bufs = (nz_even, nz_odd)
    a_bufs = (a_even, a_odd)

    @pl.when(i == 0)
    def _():
        lane = lax.broadcasted_iota(jnp.int32, (1, LANES), 1)
        even = jnp.where(lane < HEAD_DIM, 1.0, 0.0).astype(BF16)
        odd = jnp.where(lane < HEAD_DIM, 0.0, 1.0).astype(BF16)
        k3 = k_ref[...].reshape(nkb, bk, LANES)
        kst[:, 0:bk, 0:LANES] = k3 * even
        kst[:, bk:2 * bk, 0:LANES] = k3 * odd
        kst[:, 0:bk, LANES:] = jnp.broadcast_to(bias_ref[0:1, :][None], (nkb, bk, LANES)).astype(BF16)
        kst[:, bk:2 * bk, LANES:] = jnp.broadcast_to(bias_ref[1:2, :][None], (nkb, bk, LANES)).astype(BF16)
        v3 = v_ref[...].reshape(nkb, bk, LANES)
        vst[:, 0:bk, :] = v3 * even
        vst[:, bk:2 * bk, :] = v3 * odd
        r = lax.broadcasted_iota(jnp.int32, (bk, bk), 0)
        c = lax.broadcasted_iota(jnp.int32, (bk, bk), 1)
        u2[...] = jnp.where(r > c, 1.0, 0.0).astype(BF16)

    ones = jnp.where(lax.broadcasted_iota(jnp.int32, (bq, LANES), 1) < BIAS_PIECES, 1.0, 0.0).astype(BF16)
    q_ext = jnp.concatenate([q_ref[...], ones], axis=1)
    o_acc[...] = jnp.zeros_like(o_acc)
    c_acc[...] = jnp.zeros_like(c_acc)

    def scores(j):
        return lax.dot_general(q_ext, kst[j], (((1,), (1,)), ((), ())), preferred_element_type=F32)

    def weights(nz, mask):
        lsn = _log2_one_minus_beta(nz)
        if mask is not None:
            lsn = jnp.where(mask, lsn, 0.0)
        log2_beta = lsn - nz
        stacked = jnp.concatenate([lsn[:, :bk], lsn[:, bk:]], axis=0).astype(BF16)
        suf = jnp.dot(stacked, u2[...], preferred_element_type=F32)
        suffix = jnp.concatenate([suf[:bq], suf[bq:]], axis=1)
        c = c_acc[...]
        carry = jnp.concatenate([c[:, :LANES]] * reps + [c[:, LANES:]] * reps, axis=1)
        a = jnp.exp2(log2_beta + suffix + carry)
        if mask is not None:
            a = jnp.where(mask, a, 0.0)
        t_even = jnp.sum(lsn[:, :bk], axis=1, keepdims=True)
        t_odd = jnp.sum(lsn[:, bk:], axis=1, keepdims=True)
        c_acc[...] = c + jnp.concatenate(
            [jnp.broadcast_to(t_even, (bq, LANES)), jnp.broadcast_to(t_odd, (bq, LANES))], axis=1)
        return a.astype(BF16)

    def accumulate(a, j):
        o_acc[...] += jnp.dot(a, vst[j], preferred_element_type=F32)

    row = lax.broadcasted_iota(jnp.int32, (bq, 2 * bk), 0)
    col = lax.broadcasted_iota(jnp.int32, (bq, 2 * bk), 1) & (bk - 1)
    n_full = i * per_tile
    for d in reversed(range(per_tile)):
        a = weights(scores(n_full + d), col + d * bk < row)
        if d > 0:
            accumulate(a, n_full + d)

    a_bufs[0][...] = a
    nz_bufs[0][...] = scores(jnp.maximum(n_full - 1, 0))

    def body(t, _):
        for half in range(2):
            j = n_full - 1 - 2 * t - half
            nz = nz_bufs[half][...]
            nz_bufs[1 - half][...] = scores(jnp.maximum(j - 1, 0))
            accumulate(a_bufs[half][...], j + 1)
            a_bufs[1 - half][...] = weights(nz, None)
        return 0

    assert per_tile % 2 == 0
    lax.fori_loop(0, i * (per_tile // 2), body, 0)
    accumulate(a_bufs[0][...], 0)
    o_ref[...] = o_acc[...].astype(o_ref.dtype)


def _bf16_pieces(x):
    pieces = []
    for _ in range(BIAS_PIECES):
        p = x.astype(BF16).astype(F32)
        pieces.append(p)
        x = x - p
    return jnp.stack(pieces, axis=-1)


def _attention_prompt(q, kb, vb, bias2):
    b, s, _ = q.shape
    bq, bk = ATT_BQ, ATT_BK
    n_pairs = N_HEADS // 2
    offs = jnp.pad(_bf16_pieces(bias2), ((0, 0), (0, LANES - BIAS_PIECES))).reshape(n_pairs, 2, LANES)
    kern = functools.partial(_attn_prompt_kernel, bq=bq, bk=bk, seq=s)
    return pl.pallas_call(
        kern,
        grid=(b, n_pairs, s // bq),
        in_specs=[pl.BlockSpec((None, bq, LANES), lambda b_, p, i: (b_, i, p)),
                  pl.BlockSpec((None, s, LANES), lambda b_, p, i: (b_, 0, p)),
                  pl.BlockSpec((None, s, LANES), lambda b_, p, i: (b_, 0, p)),
                  pl.BlockSpec((None, 2, LANES), lambda b_, p, i: (p, 0, 0))],
        out_specs=pl.BlockSpec((None, bq, LANES), lambda b_, p, i: (b_, i, p)),
        out_shape=jax.ShapeDtypeStruct((b, s, ATT_WIDTH), BF16),
        scratch_shapes=[pltpu.VMEM((s // bk, 2 * bk, 2 * LANES), BF16),
                        pltpu.VMEM((s // bk, 2 * bk, LANES), BF16),
                        pltpu.VMEM((bk, bk), BF16),
                        pltpu.VMEM((bq, LANES), F32),
                        pltpu.VMEM((bq, 2 * LANES), F32),
                        pltpu.VMEM((bq, 2 * bk), F32),
                        pltpu.VMEM((bq, 2 * bk), F32),
                        pltpu.VMEM((bq, 2 * bk), BF16),
                        pltpu.VMEM((bq, 2 * bk), BF16)],
        compiler_params=pltpu.CompilerParams(dimension_semantics=("arbitrary",) * 3, vmem_limit_bytes=VMEM_LIMIT),
        name="attention_prompt",
    )(q, kb, vb, offs)


def _attn_sample_kernel(pt_ref, q_ref, kn_ref, vn_ref, bias_ref, *rest, n_pages):
    del pt_ref
    k_refs = rest[:n_pages]
    v_refs = rest[n_pages:2 * n_pages]
    o_ref = rest[2 * n_pages]
    kn_scr, vn_scr, u_scr = rest[2 * n_pages + 1:]
    rows = N_HEADS * SUBLANES

    @pl.when(pl.program_id(0) == 0)
    def _():
        kn_scr[...] = jnp.zeros_like(kn_scr)
        vn_scr[...] = jnp.zeros_like(vn_scr)
        r = lax.broadcasted_iota(jnp.int32, (PAGE_SIZE, PAGE_SIZE), 0)
        c = lax.broadcasted_iota(jnp.int32, (PAGE_SIZE, PAGE_SIZE), 1)
        u_scr[...] = jnp.where(r > c, 1.0, 0.0).astype(BF16)

    kn_scr[:, 0:SUBLANES, :] = kn_ref[...]
    vn_scr[:, 0:SUBLANES, :] = vn_ref[...]
    q = q_ref[...]
    bias = bias_ref[...]

    def by_head(ref):
        flat = ref.reshape(PAGE_SIZE * N_HEADS, HEAD_DIM)
        return jnp.stack([flat[pl.ds(h, PAGE_SIZE, stride=N_HEADS), :] for h in range(N_HEADS)]).astype(BF16)

    def block(kh, vh, carry, acc, mask):
        nz = jnp.einsum("htd,hsd->hts", q, kh, preferred_element_type=F32).reshape(rows, PAGE_SIZE) + bias
        lsn = _log2_one_minus_beta(nz)
        if mask is not None:
            lsn = jnp.where(mask, lsn, 0.0)
        suffix = jnp.dot(lsn.astype(BF16), u_scr[...], preferred_element_type=F32)
        a = jnp.exp2(lsn - nz + suffix + carry)
        if mask is not None:
            a = jnp.where(mask, a, 0.0)
        a3 = a.reshape(N_HEADS, SUBLANES, PAGE_SIZE).astype(BF16)
        acc = acc + jnp.einsum("hts,hsd->htd", a3, vh, preferred_element_type=F32)
        return carry + jnp.sum(lsn, axis=1, keepdims=True), acc

    t_new = lax.broadcasted_iota(jnp.int32, (rows, PAGE_SIZE), 0) & (SUBLANES - 1)
    j_new = lax.broadcasted_iota(jnp.int32, (rows, PAGE_SIZE), 1)
    carry = jnp.zeros((rows, 1), F32)
    acc = jnp.zeros((N_HEADS, SUBLANES, HEAD_DIM), F32)
    carry, acc = block(kn_scr[...].astype(BF16), vn_scr[...].astype(BF16), carry, acc, j_new < t_new)
    for p in reversed(range(n_pages)):
        carry, acc = block(by_head(k_refs[p]), by_head(v_refs[p]), carry, acc, None)
    o_ref[...] = acc


def _attention_sample(q, k_new, v_new, cache_k, cache_v, layer, page_table, bias2):
    db, t, _ = q.shape
    n_pages = page_table.shape[1]
    rows = N_HEADS * SUBLANES

    def heads_first(a):
        a = a.reshape(db, t, N_HEADS, HEAD_DIM).transpose(0, 2, 1, 3)
        return jnp.pad(a, ((0, 0), (0, 0), (0, SUBLANES - t), (0, 0)))

    bias_rows = jnp.broadcast_to(jnp.repeat(bias2, SUBLANES)[:, None], (rows, PAGE_SIZE))
    small = lambda: pl.BlockSpec((None, N_HEADS, SUBLANES, HEAD_DIM), lambda b, pt: (b, 0, 0, 0))
    page = lambda i: pl.BlockSpec((None, None, PAGE_SIZE, N_HEADS, HEAD_DIM),
                                  lambda b, pt, i=i: (layer, pt[b, i], 0, 0, 0))
    grid_spec = pltpu.PrefetchScalarGridSpec(
        num_scalar_prefetch=1,
        grid=(db,),
        in_specs=[small(), small(), small(), pl.BlockSpec((rows, PAGE_SIZE), lambda b, pt: (0, 0))]
                 + [page(i) for i in range(n_pages)] * 2,
        out_specs=small(),
        scratch_shapes=[pltpu.VMEM((N_HEADS, PAGE_SIZE, HEAD_DIM), F32),
                        pltpu.VMEM((N_HEADS, PAGE_SIZE, HEAD_DIM), F32),
                        pltpu.VMEM((PAGE_SIZE, PAGE_SIZE), BF16)],
    )
    out = pl.pallas_call(
        functools.partial(_attn_sample_kernel, n_pages=n_pages),
        grid_spec=grid_spec,
        out_shape=jax.ShapeDtypeStruct((db, N_HEADS, SUBLANES, HEAD_DIM), F32),
        compiler_params=pltpu.CompilerParams(dimension_semantics=("arbitrary",), vmem_limit_bytes=VMEM_LIMIT),
        name="attention_sample",
    )(page_table, heads_first(q), heads_first(k_new), heads_first(v_new), bias_rows,
      *([cache_k] * n_pages), *([cache_v] * n_pages))
    return out[:, :, :t, :].transpose(0, 2, 1, 3).reshape(db * t, ATT_WIDTH).astype(BF16)


def _merge_kernel(att_ref, u_ref, sv_ref, gate_ref, x_ref, woa_ref, wos_ref, wo_ref, wmix_ref, bmix_ref,
                  o_ref, mix_scr, *, chunk_len, tm):
    @pl.when(pl.program_id(0) == 0)
    def _():
        shift = chunk_len.bit_length() - 1
        r = lax.broadcasted_iota(jnp.int32, (CHUNK, CHUNK), 0)
        c = lax.broadcasted_iota(jnp.int32, (CHUNK, CHUNK), 1)
        keep = jnp.logical_and((r >> shift) == (c >> shift), c <= r)
        mix_scr[...] = jnp.where(keep[None], wmix_ref[...], 0.0).astype(BF16)

    first_group = lax.broadcasted_iota(jnp.int32, (1, LANES), 1) < SG_GROUP_DIM
    pieces = []
    for ch in range(tm // CHUNK):
        rows = slice(ch * CHUNK, (ch + 1) * CHUNK)
        svc = sv_ref[rows, :].astype(BF16)
        cols = []
        for p in range(SG_WIDTH // LANES):
            blk = svc[:, p * LANES:(p + 1) * LANES]
            m0 = jnp.dot(mix_scr[2 * p], blk, preferred_element_type=F32)
            m1 = jnp.dot(mix_scr[2 * p + 1], blk, preferred_element_type=F32)
            cols.append(jnp.where(first_group, m0, m1))
        mixed = jnp.concatenate(cols, axis=1) + bmix_ref[...]
        pieces.append(u_ref[rows, :] * mixed)
    sg = jnp.concatenate(pieces, axis=0).astype(BF16)
    merged = (gate_ref[:, :D_MODEL] * jnp.dot(att_ref[...], woa_ref[...], preferred_element_type=F32)
              + gate_ref[:, D_MODEL:] * jnp.dot(sg, wos_ref[...], preferred_element_type=F32))
    o_ref[...] = x_ref[...] + jnp.dot(merged.astype(BF16), wo_ref[...], preferred_element_type=F32)


def _merge(att, u, sv, gates, x, woa_bf, wos_bf, wo_bf, w_sg, b_sg, chunk_len):
    n = x.shape[0]
    tm = min(TOKEN_TILE, n)
    reps = CHUNK // chunk_len
    wmix = jnp.tile(w_sg[:, :chunk_len, :chunk_len], (1, reps, reps))
    bmix = jnp.repeat(jnp.tile(b_sg[:, :chunk_len].T, (reps, 1)), SG_GROUP_DIM, axis=1)
    row = lambda w: pl.BlockSpec((tm, w), lambda i: (i, 0))
    return pl.pallas_call(
        functools.partial(_merge_kernel, chunk_len=chunk_len, tm=tm),
        grid=(n // tm,),
        in_specs=[row(ATT_WIDTH), row(SG_WIDTH), row(SG_WIDTH), row(2 * D_MODEL), row(D_MODEL),
                  _const_spec((ATT_WIDTH, D_MODEL)), _const_spec((SG_WIDTH, D_MODEL)),
                  _const_spec((D_MODEL, D_MODEL)), _const_spec((SG_GROUPS, CHUNK, CHUNK)),
                  _const_spec((CHUNK, SG_WIDTH))],
        out_specs=row(D_MODEL),
        out_shape=jax.ShapeDtypeStruct((n, D_MODEL), F32),
        scratch_shapes=[pltpu.VMEM((SG_GROUPS, CHUNK, CHUNK), BF16)],
        compiler_params=pltpu.CompilerParams(dimension_semantics=("arbitrary",), vmem_limit_bytes=VMEM_LIMIT),
        name="merge",
    )(att, u, sv, gates, x, woa_bf, wos_bf, wo_bf, wmix, bmix)


def _mlp_kernel(x_ref, g_ref, w1_ref, w2_ref, gf_ref, o_ref, *, final):
    x = x_ref[...]
    xn = _rms(x, g_ref[...]).astype(BF16)
    acc = x
    for c in range(D_FF // FF_CHUNK):
        cols = slice(c * FF_CHUNK, (c + 1) * FF_CHUNK)
        h = jnp.maximum(jnp.dot(xn, w1_ref[:, cols], preferred_element_type=F32), 0.0)
        acc = acc + jnp.dot((h * h).astype(BF16), w2_ref[cols, :], preferred_element_type=F32)
    if final:
        acc = _rms(acc, gf_ref[...])
    o_ref[...] = acc


def _mlp(x, g, w1_bf, w2_bf, g_final, final):
    n = x.shape[0]
    tm = min(TOKEN_TILE, n)
    row = pl.BlockSpec((tm, D_MODEL), lambda i: (i, 0))
    return pl.pallas_call(
        functools.partial(_mlp_kernel, final=final),
        grid=(n // tm,),
        in_specs=[row, _const_spec((1, D_MODEL)), _const_spec((D_MODEL, D_FF)), _const_spec((D_FF, D_MODEL)),
                  _const_spec((1, D_MODEL))],
        out_specs=row,
        out_shape=jax.ShapeDtypeStruct((n, D_MODEL), F32),
        compiler_params=pltpu.CompilerParams(dimension_semantics=("arbitrary",), vmem_limit_bytes=VMEM_LIMIT),
        name="mlp",
    )(x, g.reshape(1, -1), w1_bf, w2_bf, g_final.reshape(1, -1))


def kernel(x_prompt, x_sample, cache_k, cache_v, page_table, g_mix, w_in, b_gate, b_sb, g_sv, w_sg, b_sg,
           w_o_att, w_o_sg, w_o, g_ffn, w_ff1, w_ff2, g_final):
    batch, seq, _ = x_prompt.shape
    dec_batch, dec_seq, _ = x_sample.shape
    depth = w_in.shape[0]
    n_pool = cache_k.shape[1]
    xp = x_prompt.reshape(batch * seq, D_MODEL)
    xs = x_sample.reshape(dec_batch * dec_seq, D_MODEL)
    ck, cv = cache_k, cache_v
    kp_l, vp_l, ks_l, vs_l, sgv_l = [], [], [], [], []
    for l in range(depth):
        w_in_bf = w_in[l].astype(BF16)
        woa_bf, wos_bf, wo_bf = w_o_att[l].astype(BF16), w_o_sg[l].astype(BF16), w_o[l].astype(BF16)
        w1_bf, w2_bf = w_ff1[l].astype(BF16), w_ff2[l].astype(BF16)
        bias2 = -b_sb[l] * LOG2E
        final = l == depth - 1

        q, k, v, kb, vb, u, sv, gates = _in_projection(xp, g_mix[l], w_in_bf, b_gate[l], g_sv[l])
        att = _attention_prompt(q.reshape(batch, seq, ATT_WIDTH), kb.reshape(batch, seq, ATT_WIDTH),
                                vb.reshape(batch, seq, ATT_WIDTH), bias2).reshape(batch * seq, ATT_WIDTH)
        xp = _merge(att, u, sv, gates, xp, woa_bf, wos_bf, wo_bf, w_sg[l], b_sg[l], CHUNK)
        kp_l.append(k.reshape(batch, seq, N_HEADS, HEAD_DIM))
        vp_l.append(v.reshape(batch, seq, N_HEADS, HEAD_DIM))

        q, k, v, _, _, u, sv, gates = _in_projection(xs, g_mix[l], w_in_bf, b_gate[l], g_sv[l])
        shp = (dec_batch, dec_seq, ATT_WIDTH)
        att = _attention_sample(q.reshape(shp), k.reshape(shp), v.reshape(shp), ck, cv, l, page_table, bias2)
        xs = _merge(att, u, sv, gates, xs, woa_bf, wos_bf, wo_bf, w_sg[l], b_sg[l], dec_seq)
        ks_l.append(k.reshape(dec_batch, dec_seq, N_HEADS, HEAD_DIM))
        vs_l.append(v.reshape(dec_batch, dec_seq, N_HEADS, HEAD_DIM))
        sgv_l.append(sv.reshape(dec_batch, dec_seq, SG_GROUPS, SG_GROUP_DIM))

        xp = _mlp(xp, g_ffn[l], w1_bf, w2_bf, g_final, final)
        xs = _mlp(xs, g_ffn[l], w1_bf, w2_bf, g_final, final)

    return (xp.reshape(batch, seq, D_MODEL), xs.reshape(dec_batch, dec_seq, D_MODEL),
            jnp.stack(kp_l), jnp.stack(vp_l), jnp.stack(ks_l), jnp.stack(vs_l), jnp.stack(sgv_l))
```

```python
import functools
import math

import jax
import jax.numpy as jnp
from jax import lax
from jax.experimental import pallas as pl
from jax.experimental.pallas import tpu as pltpu

F32 = jnp.float32
BF16 = jnp.bfloat16

D_MODEL = 1024
N_HEADS = 8
HEAD_DIM = 64
ATT_WIDTH = N_HEADS * HEAD_DIM
SG_GROUPS = 8
SG_GROUP_DIM = 64
SG_WIDTH = SG_GROUPS * SG_GROUP_DIM
CHUNK = 128
D_FF = 4 * D_MODEL
PAGE_SIZE = 128
EPS = 1e-6
IN_WIDTH = 3 * ATT_WIDTH + 2 * SG_WIDTH + 2 * D_MODEL

LANES = 128
SUBLANES = 8
LOG2E = math.log2(math.e)
Q_SCALE = -(HEAD_DIM ** -0.5) * LOG2E

TOKEN_TILE = 512
ATT_BQ = 512
ATT_BK = 256
FF_CHUNK = 1024
BIAS_PIECES = 3
VMEM_LIMIT = 56 * 1024 * 1024


def _rms(x, g):
    return x * lax.rsqrt(jnp.mean(x * x, axis=-1, keepdims=True) + EPS) * g


def _const_spec(shape):
    return pl.BlockSpec(shape, lambda *_: (0,) * len(shape), pipeline_mode=pl.Buffered(1))


def _log2_one_minus_beta(nz):
    return jnp.minimum(nz, 0.0) - jnp.log(1.0 + jnp.exp2(-jnp.abs(nz))) * LOG2E


def _inproj_kernel(x_ref, g_ref, w_ref, bg_ref, gsv_ref,
                   q_ref, k_ref, v_ref, kb_ref, vb_ref, u_ref, sv_ref, gate_ref):
    xn = _rms(x_ref[...], g_ref[...]).astype(BF16)

    def proj(lo, hi):
        return jnp.dot(xn, w_ref[:, lo:hi], preferred_element_type=F32)

    o_k, o_v, o_u, o_sv, o_g = ATT_WIDTH, 2 * ATT_WIDTH, 3 * ATT_WIDTH, 3 * ATT_WIDTH + SG_WIDTH, 3 * ATT_WIDTH + 2 * SG_WIDTH
    q_ref[...] = (proj(0, o_k) * Q_SCALE).astype(BF16)
    k = proj(o_k, o_v)
    k_ref[...] = k
    kb_ref[...] = k.astype(BF16)
    v = proj(o_v, o_u)
    v_ref[...] = v
    vb_ref[...] = v.astype(BF16)
    u_ref[...] = jax.nn.gelu(proj(o_u, o_sv))
    sv_ref[...] = _rms(jax.nn.gelu(proj(o_sv, o_g)), gsv_ref[...])
    gate_ref[...] = jax.nn.sigmoid(proj(o_g, IN_WIDTH) + bg_ref[...])


def _in_projection(x, g, w_bf, b_gate, g_sv):
    n = x.shape[0]
    tm = min(TOKEN_TILE, n)
    row = lambda w: pl.BlockSpec((tm, w), lambda i: (i, 0))
    return pl.pallas_call(
        _inproj_kernel,
        grid=(n // tm,),
        in_specs=[row(D_MODEL), _const_spec((1, D_MODEL)), _const_spec((D_MODEL, IN_WIDTH)),
                  _const_spec((1, 2 * D_MODEL)), _const_spec((1, SG_WIDTH))],
        out_specs=[row(ATT_WIDTH)] * 5 + [row(SG_WIDTH)] * 2 + [row(2 * D_MODEL)],
        out_shape=[jax.ShapeDtypeStruct((n, ATT_WIDTH), BF16),
                   jax.ShapeDtypeStruct((n, ATT_WIDTH), F32),
                   jax.ShapeDtypeStruct((n, ATT_WIDTH), F32),
                   jax.ShapeDtypeStruct((n, ATT_WIDTH), BF16),
                   jax.ShapeDtypeStruct((n, ATT_WIDTH), BF16),
                   jax.ShapeDtypeStruct((n, SG_WIDTH), F32),
                   jax.ShapeDtypeStruct((n, SG_WIDTH), F32),
                   jax.ShapeDtypeStruct((n, 2 * D_MODEL), F32)],
        compiler_params=pltpu.CompilerParams(dimension_semantics=("arbitrary",), vmem_limit_bytes=VMEM_LIMIT),
        name="in_projection",
    )(x, g.reshape(1, -1), w_bf, b_gate.reshape(1, -1), g_sv.reshape(1, -1))


def _attn_prompt_kernel(q_ref, k_ref, v_ref, bias_ref, o_ref, kst, vst, u2, o_acc, c_acc,
                        nz_even, nz_odd, a_even, a_odd, *, bq, bk, seq):
    i = pl.program_id(2)
    nkb = seq // bk
    per_tile = bq // bk
    reps = bk // LANES
    nz_bufs = (nz_even, nz_odd)
    a_bufs = (a_even, a_odd)

    @pl.when(i == 0)
    def _():
        lane = lax.broadcasted_iota(jnp.int32, (1, LANES), 1)
        even = jnp.where(lane < HEAD_DIM, 1.0, 0.0).astype(BF16)
        odd = jnp.where(lane < HEAD_DIM, 0.0, 1.0).astype(BF16)
        k3 = k_ref[...].reshape(nkb, bk, LANES)
        kst[:, 0:bk, 0:LANES] = k3 * even
        kst[:, bk:2 * bk, 0:LANES] = k3 * odd
        kst[:, 0:bk, LANES:] = jnp.broadcast_to(bias_ref[0:1, :][None], (nkb, bk, LANES)).astype(BF16)
        kst[:, bk:2 * bk, LANES:] = jnp.broadcast_to(bias_ref[1:2, :][None], (nkb, bk, LANES)).astype(BF16)
        v3 = v_ref[...].reshape(nkb, bk, LANES)
        vst[:, 0:bk, :] = v3 * even
        vst[:, bk:2 * bk, :] = v3 * odd
        r = lax.broadcasted_iota(jnp.int32, (bk, bk), 0)
        c = lax.broadcasted_iota(jnp.int32, (bk, bk), 1)
        u2[...] = jnp.where(r > c, 1.0, 0.0).astype(BF16)

    ones = jnp.where(lax.broadcasted_iota(jnp.int32, (bq, LANES), 1) < BIAS_PIECES, 1.0, 0.0).astype(BF16)
    q_ext = jnp.concatenate([q_ref[...], ones], axis=1)
    o_acc[...] = jnp.zeros_like(o_acc)
    c_acc[...] = jnp.zeros_like(c_acc)

    def scores(j):
        return lax.dot_general(q_ext, kst[j], (((1,), (1,)), ((), ())), preferred_element_type=F32)

    def weights(nz, mask):
        lsn = _log2_one_minus_beta(nz)
        if mask is not None:
            lsn = jnp.where(mask, lsn, 0.0)
        log2_beta = lsn - nz
        stacked = jnp.concatenate([lsn[:, :bk], lsn[:, bk:]], axis=0).astype(BF16)
        suf = jnp.dot(stacked, u2[...], preferred_element_type=F32)
        suffix = jnp.concatenate([suf[:bq], suf[bq:]], axis=1)
        c = c_acc[...]
        carry = jnp.concatenate([c[:, :LANES]] * reps + [c[:, LANES:]] * reps, axis=1)
        a = jnp.exp2(log2_beta + suffix + carry)
        if mask is not None:
            a = jnp.where(mask, a, 0.0)
        t_even = jnp.sum(lsn[:, :bk], axis=1, keepdims=True)
        t_odd = jnp.sum(lsn[:, bk:], axis=1, keepdims=True)
        c_acc[...] = c + jnp.concatenate(
            [jnp.broadcast_to(t_even, (bq, LANES)), jnp.broadcast_to(t_odd, (bq, LANES))], axis=1)
        return a.astype(BF16)

    def accumulate(a, j):
        o_acc[...] += jnp.dot(a, vst[j], preferred_element_type=F32)

    row = lax.broadcasted_iota(jnp.int32, (bq, 2 * bk), 0)
    col = lax.broadcasted_iota(jnp.int32, (bq, 2 * bk), 1) & (bk - 1)
    n_full = i * per_tile
    for d in reversed(range(per_tile)):
        a = weights(scores(n_full + d), col + d * bk < row)
        if d > 0:
            accumulate(a, n_full + d)

    a_bufs[0][...] = a
    nz_bufs[0][...] = scores(jnp.maximum(n_full - 1, 0))

    def body(t, _):
        for half in range(2):
            j = n_full - 1 - 2 * t - half
            nz = nz_bufs[half][...]
            nz_bufs[1 - half][...] = scores(jnp.maximum(j - 1, 0))
            accumulate(a_bufs[half][...], j + 1)
            a_bufs[1 - half][...] = weights(nz, None)
        return 0

    assert per_tile % 2 == 0
    lax.fori_loop(0, i * (per_tile // 2), body, 0)
    accumulate(a_bufs[0][...], 0)
    o_ref[...] = o_acc[...].astype(o_ref.dtype)


def _bf16_pieces(x):
    pieces = []
    for _ in range(BIAS_PIECES):
        p = x.astype(BF16).astype(F32)
        pieces.append(p)
        x = x - p
    return jnp.stack(pieces, axis=-1)


def _attention_prompt(q, kb, vb, bias2):
    b, s, _ = q.shape
    bq, bk = ATT_BQ, ATT_BK
    n_pairs = N_HEADS // 2
    offs = jnp.pad(_bf16_pieces(bias2), ((0, 0), (0, LANES - BIAS_PIECES))).reshape(n_pairs, 2, LANES)
    kern = functools.partial(_attn_prompt_kernel, bq=bq, bk=bk, seq=s)
    return pl.pallas_call(
        kern,
        grid=(b, n_pairs, s // bq),
        in_specs=[pl.BlockSpec((None, bq, LANES), lambda b_, p, i: (b_, i, p)),
                  pl.BlockSpec((None, s, LANES), lambda b_, p, i: (b_, 0, p)),
                  pl.BlockSpec((None, s, LANES), lambda b_, p, i: (b_, 0, p)),
                  pl.BlockSpec((None, 2, LANES), lambda b_, p, i: (p, 0, 0))],
        out_specs=pl.BlockSpec((None, bq, LANES), lambda b_, p, i: (b_, i, p)),
        out_shape=jax.ShapeDtypeStruct((b, s, ATT_WIDTH), BF16),
        scratch_shapes=[pltpu.VMEM((s // bk, 2 * bk, 2 * LANES), BF16),
                        pltpu.VMEM((s // bk, 2 * bk, LANES), BF16),
                        pltpu.VMEM((bk, bk), BF16),
                        pltpu.VMEM((bq, LANES), F32),
                        pltpu.VMEM((bq, 2 * LANES), F32),
                        pltpu.VMEM((bq, 2 * bk), F32),
                        pltpu.VMEM((bq, 2 * bk), F32),
                        pltpu.VMEM((bq, 2 * bk), BF16),
                        pltpu.VMEM((bq, 2 * bk), BF16)],
        compiler_params=pltpu.CompilerParams(dimension_semantics=("arbitrary",) * 3, vmem_limit_bytes=VMEM_LIMIT),
        name="attention_prompt",
    )(q, kb, vb, offs)


def _attn_sample_kernel(pt_ref, q_ref, kn_ref, vn_ref, bias_ref, *rest, n_pages):
    del pt_ref
    k_refs = rest[:n_pages]
    v_refs = rest[n_pages:2 * n_pages]
    o_ref = rest[2 * n_pages]
    kn_scr, vn_scr, u_scr = rest[2 * n_pages + 1:]
    rows = N_HEADS * SUBLANES

    @pl.when(pl.program_id(0) == 0)
    def _():
        kn_scr[...] = jnp.zeros_like(kn_scr)
        vn_scr[...] = jnp.zeros_like(vn_scr)
        r = lax.broadcasted_iota(jnp.int32, (PAGE_SIZE, PAGE_SIZE), 0)
        c = lax.broadcasted_iota(jnp.int32, (PAGE_SIZE, PAGE_SIZE), 1)
        u_scr[...] = jnp.where(r > c, 1.0, 0.0).astype(BF16)

    kn_scr[:, :, 0:SUBLANES] = kn_ref[...]
    vn_scr[:, :, 0:SUBLANES] = vn_ref[...]
    q = q_ref[...]
    bias = bias_ref[...]

    k_blocks = [kn_scr] + [k_refs[p] for p in reversed(range(n_pages))]
    v_blocks = [vn_scr] + [v_refs[p] for p in reversed(range(n_pages))]
    t_new = lax.broadcasted_iota(jnp.int32, (rows, PAGE_SIZE), 0) & (SUBLANES - 1)
    j_new = lax.broadcasted_iota(jnp.int32, (rows, PAGE_SIZE), 1)
    new_mask = j_new < t_new

    nzs = [jnp.einsum("htd,hds->hts", q, kb[...].astype(BF16), preferred_element_type=F32
                      ).reshape(rows, PAGE_SIZE) + bias for kb in k_blocks]
    lsns = [_log2_one_minus_beta(nz) for nz in nzs]
    lsns[0] = jnp.where(new_mask, lsns[0], 0.0)
    suffix = jnp.dot(jnp.concatenate(lsns, axis=0).astype(BF16), u_scr[...], preferred_element_type=F32)
    carry = jnp.zeros((rows, 1), F32)
    acc = jnp.zeros((N_HEADS, SUBLANES, HEAD_DIM), F32)
    for b, (nz, lsn, vb) in enumerate(zip(nzs, lsns, v_blocks)):
        a = jnp.exp2(lsn - nz + suffix[b * rows:(b + 1) * rows] + carry)
        if b == 0:
            a = jnp.where(new_mask, a, 0.0)
        a3 = a.reshape(N_HEADS, SUBLANES, PAGE_SIZE).astype(BF16)
        acc = acc + jnp.einsum("hts,hds->htd", a3, vb[...].astype(BF16), preferred_element_type=F32)
        carry = carry + jnp.sum(lsn, axis=1, keepdims=True)
    o_ref[...] = acc


def _attention_sample(q, k_new, v_new, cache_k, cache_v, layer, page_table, bias2):
    db, t, _ = q.shape
    n_pages = page_table.shape[1]
    rows = N_HEADS * SUBLANES

    def heads_first(a):
        a = a.reshape(db, t, N_HEADS, HEAD_DIM).transpose(0, 2, 1, 3)
        return jnp.pad(a, ((0, 0), (0, 0), (0, SUBLANES - t), (0, 0)))

    def keys_last(a):
        a = a.reshape(db, t, N_HEADS, HEAD_DIM).transpose(0, 2, 3, 1)
        return jnp.pad(a, ((0, 0), (0, 0), (0, 0), (0, SUBLANES - t)))

    bias_rows = jnp.broadcast_to(jnp.repeat(bias2, SUBLANES)[:, None], (rows, PAGE_SIZE))
    small = lambda: pl.BlockSpec((None, N_HEADS, SUBLANES, HEAD_DIM), lambda b, pt: (b, 0, 0, 0))
    new_keys = lambda: pl.BlockSpec((None, N_HEADS, HEAD_DIM, SUBLANES), lambda b, pt: (b, 0, 0, 0))
    page = lambda i: pl.BlockSpec((None, None, N_HEADS, HEAD_DIM, PAGE_SIZE),
                                  lambda b, pt, i=i: (layer, pt[b, i], 0, 0, 0))
    grid_spec = pltpu.PrefetchScalarGridSpec(
        num_scalar_prefetch=1,
        grid=(db,),
        in_specs=[small(), new_keys(), new_keys(), pl.BlockSpec((rows, PAGE_SIZE), lambda b, pt: (0, 0))]
                 + [page(i) for i in range(n_pages)] * 2,
        out_specs=small(),
        scratch_shapes=[pltpu.VMEM((N_HEADS, HEAD_DIM, PAGE_SIZE), F32),
                        pltpu.VMEM((N_HEADS, HEAD_DIM, PAGE_SIZE), F32),
                        pltpu.VMEM((PAGE_SIZE, PAGE_SIZE), BF16)],
    )
    out = pl.pallas_call(
        functools.partial(_attn_sample_kernel, n_pages=n_pages),
        grid_spec=grid_spec,
        out_shape=jax.ShapeDtypeStruct((db, N_HEADS, SUBLANES, HEAD_DIM), F32),
        compiler_params=pltpu.CompilerParams(dimension_semantics=("arbitrary",), vmem_limit_bytes=VMEM_LIMIT),
        name="attention_sample",
    )(page_table, heads_first(q), keys_last(k_new), keys_last(v_new), bias_rows,
      *([cache_k] * n_pages), *([cache_v] * n_pages))
    return out[:, :, :t, :].transpose(0, 2, 1, 3).reshape(db * t, ATT_WIDTH).astype(BF16)


def _merge_kernel(att_ref, u_ref, sv_ref, gate_ref, x_ref, woa_ref, wos_ref, wo_ref, wmix_ref, bmix_ref,
                  o_ref, mix_scr, *, chunk_len, tm):
    @pl.when(pl.program_id(0) == 0)
    def _():
        shift = chunk_len.bit_length() - 1
        r = lax.broadcasted_iota(jnp.int32, (CHUNK, CHUNK), 0)
        c = lax.broadcasted_iota(jnp.int32, (CHUNK, CHUNK), 1)
        keep = jnp.logical_and((r >> shift) == (c >> shift), c <= r)
        mix_scr[...] = jnp.where(keep[None], wmix_ref[...], 0.0).astype(BF16)

    first_group = lax.broadcasted_iota(jnp.int32, (1, LANES), 1) < SG_GROUP_DIM
    pieces = []
    for ch in range(tm // CHUNK):
        rows = slice(ch * CHUNK, (ch + 1) * CHUNK)
        svc = sv_ref[rows, :].astype(BF16)
        cols = []
        for p in range(SG_WIDTH // LANES):
            blk = svc[:, p * LANES:(p + 1) * LANES]
            m0 = jnp.dot(mix_scr[2 * p], blk, preferred_element_type=F32)
            m1 = jnp.dot(mix_scr[2 * p + 1], blk, preferred_element_type=F32)
            cols.append(jnp.where(first_group, m0, m1))
        mixed = jnp.concatenate(cols, axis=1) + bmix_ref[...]
        pieces.append(u_ref[rows, :] * mixed)
    sg = jnp.concatenate(pieces, axis=0).astype(BF16)
    merged = (gate_ref[:, :D_MODEL] * jnp.dot(att_ref[...], woa_ref[...], preferred_element_type=F32)
              + gate_ref[:, D_MODEL:] * jnp.dot(sg, wos_ref[...], preferred_element_type=F32))
    o_ref[...] = x_ref[...] + jnp.dot(merged.astype(BF16), wo_ref[...], preferred_element_type=F32)


def _merge(att, u, sv, gates, x, woa_bf, wos_bf, wo_bf, w_sg, b_sg, chunk_len):
    n = x.shape[0]
    tm = min(TOKEN_TILE, n)
    reps = CHUNK // chunk_len
    wmix = jnp.tile(w_sg[:, :chunk_len, :chunk_len], (1, reps, reps))
    bmix = jnp.repeat(jnp.tile(b_sg[:, :chunk_len].T, (reps, 1)), SG_GROUP_DIM, axis=1)
    row = lambda w: pl.BlockSpec((tm, w), lambda i: (i, 0))
    return pl.pallas_call(
        functools.partial(_merge_kernel, chunk_len=chunk_len, tm=tm),
        grid=(n // tm,),
        in_specs=[row(ATT_WIDTH), row(SG_WIDTH), row(SG_WIDTH), row(2 * D_MODEL), row(D_MODEL),
                  _const_spec((ATT_WIDTH, D_MODEL)), _const_spec((SG_WIDTH, D_MODEL)),
                  _const_spec((D_MODEL, D_MODEL)), _const_spec((SG_GROUPS, CHUNK, CHUNK)),
                  _const_spec((CHUNK, SG_WIDTH))],
        out_specs=row(D_MODEL),
        out_shape=jax.ShapeDtypeStruct((n, D_MODEL), F32),
        scratch_shapes=[pltpu.VMEM((SG_GROUPS, CHUNK, CHUNK), BF16)],
        compiler_params=pltpu.CompilerParams(dimension_semantics=("arbitrary",), vmem_limit_bytes=VMEM_LIMIT),
        name="merge",
    )(att, u, sv, gates, x, woa_bf, wos_bf, wo_bf, wmix, bmix)


def _mlp_kernel(x_ref, g_ref, w1_ref, w2_ref, gf_ref, o_ref, *, final):
    x = x_ref[...]
    xn = _rms(x, g_ref[...]).astype(BF16)
    acc = x
    for c in range(D_FF // FF_CHUNK):
        cols = slice(c * FF_CHUNK, (c + 1) * FF_CHUNK)
        h = jnp.maximum(jnp.dot(xn, w1_ref[:, cols], preferred_element_type=F32), 0.0)
        acc = acc + jnp.dot((h * h).astype(BF16), w2_ref[cols, :], preferred_element_type=F32)
    if final:
        acc = _rms(acc, gf_ref[...])
    o_ref[...] = acc


def _mlp(x, g, w1_bf, w2_bf, g_final, final):
    n = x.shape[0]
    tm = min(TOKEN_TILE, n)
    row = pl.BlockSpec((tm, D_MODEL), lambda i: (i, 0))
    return pl.pallas_call(
        functools.partial(_mlp_kernel, final=final),
        grid=(n // tm,),
        in_specs=[row, _const_spec((1, D_MODEL)), _const_spec((D_MODEL, D_FF)), _const_spec((D_FF, D_MODEL)),
                  _const_spec((1, D_MODEL))],
        out_specs=row,
        out_shape=jax.ShapeDtypeStruct((n, D_MODEL), F32),
        compiler_params=pltpu.CompilerParams(dimension_semantics=("arbitrary",), vmem_limit_bytes=VMEM_LIMIT),
        name="mlp",
    )(x, g.reshape(1, -1), w1_bf, w2_bf, g_final.reshape(1, -1))


def kernel(x_prompt, x_sample, cache_k, cache_v, page_table, g_mix, w_in, b_gate, b_sb, g_sv, w_sg, b_sg,
           w_o_att, w_o_sg, w_o, g_ffn, w_ff1, w_ff2, g_final):
    batch, seq, _ = x_prompt.shape
    dec_batch, dec_seq, _ = x_sample.shape
    depth = w_in.shape[0]
    n_pool = cache_k.shape[1]
    xp = x_prompt.reshape(batch * seq, D_MODEL)
    xs = x_sample.reshape(dec_batch * dec_seq, D_MODEL)
    ck = cache_k.transpose(0, 1, 3, 4, 2)
    cv = cache_v.transpose(0, 1, 3, 4, 2)
    kp_l, vp_l, ks_l, vs_l, sgv_l = [], [], [], [], []
    for l in range(depth):
        w_in_bf = w_in[l].astype(BF16)
        woa_bf, wos_bf, wo_bf = w_o_att[l].astype(BF16), w_o_sg[l].astype(BF16), w_o[l].astype(BF16)
        w1_bf, w2_bf = w_ff1[l].astype(BF16), w_ff2[l].astype(BF16)
        bias2 = -b_sb[l] * LOG2E
        final = l == depth - 1

        q, k, v, kb, vb, u, sv, gates = _in_projection(xp, g_mix[l], w_in_bf, b_gate[l], g_sv[l])
        att = _attention_prompt(q.reshape(batch, seq, ATT_WIDTH), kb.reshape(batch, seq, ATT_WIDTH),
                                vb.reshape(batch, seq, ATT_WIDTH), bias2).reshape(batch * seq, ATT_WIDTH)
        xp = _merge(att, u, sv, gates, xp, woa_bf, wos_bf, wo_bf, w_sg[l], b_sg[l], CHUNK)
        kp_l.append(k.reshape(batch, seq, N_HEADS, HEAD_DIM))
        vp_l.append(v.reshape(batch, seq, N_HEADS, HEAD_DIM))

        q, k, v, _, _, u, sv, gates = _in_projection(xs, g_mix[l], w_in_bf, b_gate[l], g_sv[l])
        shp = (dec_batch, dec_seq, ATT_WIDTH)
        att = _attention_sample(q.reshape(shp), k.reshape(shp), v.reshape(shp), ck, cv, l, page_table, bias2)
        xs = _merge(att, u, sv, gates, xs, woa_bf, wos_bf, wo_bf, w_sg[l], b_sg[l], dec_seq)
        ks_l.append(k.reshape(dec_batch, dec_seq, N_HEADS, HEAD_DIM))
        vs_l.append(v.reshape(dec_batch, dec_seq, N_HEADS, HEAD_DIM))
        sgv_l.append(sv.reshape(dec_batch, dec_seq, SG_GROUPS, SG_GROUP_DIM))

        xp = _mlp(xp, g_ffn[l], w1_bf, w2_bf, g_final, final)
        xs = _mlp(xs, g_ffn[l], w1_bf, w2_bf, g_final, final)

    return (xp.reshape(batch, seq, D_MODEL), xs.reshape(dec_batch, dec_seq, D_MODEL),
            jnp.stack(kp_l), jnp.stack(vp_l), jnp.stack(ks_l), jnp.stack(vs_l), jnp.stack(sgv_l))
```

```python
import functools
import math

import jax
import jax.numpy as jnp
from jax import lax
from jax.experimental import pallas as pl
from jax.experimental.pallas import tpu as pltpu

F32 = jnp.float32
BF16 = jnp.bfloat16

D_MODEL = 1024
N_HEADS = 8
HEAD_DIM = 64
ATT_WIDTH = N_HEADS * HEAD_DIM
SG_GROUPS = 8
SG_GROUP_DIM = 64
SG_WIDTH = SG_GROUPS * SG_GROUP_DIM
CHUNK = 128
D_FF = 4 * D_MODEL
PAGE_SIZE = 128
EPS = 1e-6
IN_WIDTH = 3 * ATT_WIDTH + 2 * SG_WIDTH + 2 * D_MODEL

LANES = 128
SUBLANES = 8
LOG2E = math.log2(math.e)
Q_SCALE = -(HEAD_DIM ** -0.5) * LOG2E

TOKEN_TILE = 512
ATT_BQ = 1024
ATT_BK = 256
FF_CHUNK = 1024
BIAS_PIECES = 3
VMEM_LIMIT = 56 * 1024 * 1024


def _rms(x, g):
    return x * lax.rsqrt(jnp.mean(x * x, axis=-1, keepdims=True) + EPS) * g


def _const_spec(shape):
    return pl.BlockSpec(shape, lambda *_: (0,) * len(shape), pipeline_mode=pl.Buffered(1))


def _log2_one_minus_beta(nz):
    return jnp.minimum(nz, 0.0) - jnp.log(1.0 + jnp.exp2(-jnp.abs(nz))) * LOG2E


def _inproj_kernel(x_ref, g_ref, w_ref, bg_ref, gsv_ref,
                   q_ref, k_ref, v_ref, kb_ref, vb_ref, u_ref, sv_ref, gate_ref):
    xn = _rms(x_ref[...], g_ref[...]).astype(BF16)

    def proj(lo, hi):
        return jnp.dot(xn, w_ref[:, lo:hi], preferred_element_type=F32)

    o_k, o_v, o_u, o_sv, o_g = ATT_WIDTH, 2 * ATT_WIDTH, 3 * ATT_WIDTH, 3 * ATT_WIDTH + SG_WIDTH, 3 * ATT_WIDTH + 2 * SG_WIDTH
    q_ref[...] = (proj(0, o_k) * Q_SCALE).astype(BF16)
    k = proj(o_k, o_v)
    k_ref[...] = k
    kb_ref[...] = k.astype(BF16)
    v = proj(o_v, o_u)
    v_ref[...] = v
    vb_ref[...] = v.astype(BF16)
    u_ref[...] = jax.nn.gelu(proj(o_u, o_sv))
    sv_ref[...] = _rms(jax.nn.gelu(proj(o_sv, o_g)), gsv_ref[...])
    gate_ref[...] = jax.nn.sigmoid(proj(o_g, IN_WIDTH) + bg_ref[...])


def _in_projection(x, g, w_bf, b_gate, g_sv):
    n = x.shape[0]
    tm = min(TOKEN_TILE, n)
    row = lambda w: pl.BlockSpec((tm, w), lambda i: (i, 0))
    return pl.pallas_call(
        _inproj_kernel,
        grid=(n // tm,),
        in_specs=[row(D_MODEL), _const_spec((1, D_MODEL)), _const_spec((D_MODEL, IN_WIDTH)),
                  _const_spec((1, 2 * D_MODEL)), _const_spec((1, SG_WIDTH))],
        out_specs=[row(ATT_WIDTH)] * 5 + [row(SG_WIDTH)] * 2 + [row(2 * D_MODEL)],
        out_shape=[jax.ShapeDtypeStruct((n, ATT_WIDTH), BF16),
                   jax.ShapeDtypeStruct((n, ATT_WIDTH), F32),
                   jax.ShapeDtypeStruct((n, ATT_WIDTH), F32),
                   jax.ShapeDtypeStruct((n, ATT_WIDTH), BF16),
                   jax.ShapeDtypeStruct((n, ATT_WIDTH), BF16),
                   jax.ShapeDtypeStruct((n, SG_WIDTH), F32),
                   jax.ShapeDtypeStruct((n, SG_WIDTH), F32),
                   jax.ShapeDtypeStruct((n, 2 * D_MODEL), F32)],
        compiler_params=pltpu.CompilerParams(dimension_semantics=("arbitrary",), vmem_limit_bytes=VMEM_LIMIT),
        name="in_projection",
    )(x, g.reshape(1, -1), w_bf, b_gate.reshape(1, -1), g_sv.reshape(1, -1))


def _attn_prompt_kernel(q_ref, k_ref, v_ref, bias_ref, o_ref, kst, vst, u2, o_acc, c_acc,
                        nz_even, nz_odd, a_even, a_odd, *, bq, bk, seq):
    i = pl.program_id(2)
    nkb = seq // bk
    per_tile = bq // bk
    reps = bk // LANES
    nz_bufs = (nz_even, nz_odd)
    a_bufs = (a_even, a_odd)

    @pl.when(i == 0)
    def _():
        lane = lax.broadcasted_iota(jnp.int32, (1, LANES), 1)
        even = jnp.where(lane < HEAD_DIM, 1.0, 0.0).astype(BF16)
        odd = jnp.where(lane < HEAD_DIM, 0.0, 1.0).astype(BF16)
        k3 = k_ref[...].reshape(nkb, bk, LANES)
        kst[:, 0:bk, 0:LANES] = k3 * even
        kst[:, bk:2 * bk, 0:LANES] = k3 * odd
        kst[:, 0:bk, LANES:] = jnp.broadcast_to(bias_ref[0:1, :][None], (nkb, bk, LANES)).astype(BF16)
        kst[:, bk:2 * bk, LANES:] = jnp.broadcast_to(bias_ref[1:2, :][None], (nkb, bk, LANES)).astype(BF16)
        v3 = v_ref[...].reshape(nkb, bk, LANES)
        vst[:, 0:bk, :] = v3 * even
        vst[:, bk:2 * bk, :] = v3 * odd
        r = lax.broadcasted_iota(jnp.int32, (bk, bk), 0)
        c = lax.broadcasted_iota(jnp.int32, (bk, bk), 1)
        u2[...] = jnp.where(r > c, 1.0, 0.0).astype(BF16)

    ones = jnp.where(lax.broadcasted_iota(jnp.int32, (bq, LANES), 1) < BIAS_PIECES, 1.0, 0.0).astype(BF16)
    q_ext = jnp.concatenate([q_ref[...], ones], axis=1)
    o_acc[...] = jnp.zeros_like(o_acc)
    c_acc[...] = jnp.zeros_like(c_acc)

    def scores(j, r0=0):
        return lax.dot_general(q_ext[r0:], kst[j], (((1,), (1,)), ((), ())), preferred_element_type=F32)

    def weights(nz, mask, r0=0):
        n = bq - r0
        lsn = _log2_one_minus_beta(nz)
        if mask is not None:
            lsn = jnp.where(mask, lsn, 0.0)
        log2_beta = lsn - nz
        stacked = jnp.concatenate([lsn[:, :bk], lsn[:, bk:]], axis=0).astype(BF16)
        suf = jnp.dot(stacked, u2[...], preferred_element_type=F32)
        suffix = jnp.concatenate([suf[:n], suf[n:]], axis=1)
        c = c_acc[r0:, :]
        carry = jnp.concatenate([c[:, :LANES]] * reps + [c[:, LANES:]] * reps, axis=1)
        a = jnp.exp2(log2_beta + suffix + carry)
        if mask is not None:
            a = jnp.where(mask, a, 0.0)
        t_even = jnp.sum(lsn[:, :bk], axis=1, keepdims=True)
        t_odd = jnp.sum(lsn[:, bk:], axis=1, keepdims=True)
        c_acc[r0:, :] = c + jnp.concatenate(
            [jnp.broadcast_to(t_even, (n, LANES)), jnp.broadcast_to(t_odd, (n, LANES))], axis=1)
        return a.astype(BF16)

    def accumulate(a, j, r0=0):
        o_acc[r0:, :] += jnp.dot(a, vst[j], preferred_element_type=F32)

    n_full = i * per_tile
    nz = scores(n_full + per_tile - 1, (per_tile - 1) * bk)
    pending = None
    for d in reversed(range(per_tile)):
        r0 = d * bk
        if d > 0:
            nz_next = scores(n_full + d - 1, r0 - bk)
        else:
            nz_next = scores(jnp.maximum(n_full - 1, 0))
        if pending is not None:
            accumulate(*pending)
        row = lax.broadcasted_iota(jnp.int32, (bq - r0, 2 * bk), 0)
        col = lax.broadcasted_iota(jnp.int32, (bq - r0, 2 * bk), 1) & (bk - 1)
        pending = (weights(nz, col < row, r0), n_full + d, r0)
        nz = nz_next

    a_bufs[0][...] = pending[0]
    nz_bufs[0][...] = nz

    def body(t, _):
        for step in range(per_tile):
            j = n_full - 1 - per_tile * t - step
            cur, nxt = step % 2, 1 - step % 2
            nz = nz_bufs[cur][...]
            nz_bufs[nxt][...] = scores(jnp.maximum(j - 1, 0))
            accumulate(a_bufs[cur][...], j + 1)
            a_bufs[nxt][...] = weights(nz, None)
        return 0

    assert per_tile % 2 == 0
    lax.fori_loop(0, i, body, 0)
    accumulate(a_bufs[0][...], 0)
    o_ref[...] = o_acc[...].astype(o_ref.dtype)


def _bf16_pieces(x):
    pieces = []
    for _ in range(BIAS_PIECES):
        p = x.astype(BF16).astype(F32)
        pieces.append(p)
        x = x - p
    return jnp.stack(pieces, axis=-1)


def _attention_prompt(q, kb, vb, bias2):
    b, s, _ = q.shape
    bq, bk = ATT_BQ, ATT_BK
    n_pairs = N_HEADS // 2
    offs = jnp.pad(_bf16_pieces(bias2), ((0, 0), (0, LANES - BIAS_PIECES))).reshape(n_pairs, 2, LANES)
    kern = functools.partial(_attn_prompt_kernel, bq=bq, bk=bk, seq=s)
    return pl.pallas_call(
        kern,
        grid=(b, n_pairs, s // bq),
        in_specs=[pl.BlockSpec((None, bq, LANES), lambda b_, p, i: (b_, i, p)),
                  pl.BlockSpec((None, s, LANES), lambda b_, p, i: (b_, 0, p)),
                  pl.BlockSpec((None, s, LANES), lambda b_, p, i: (b_, 0, p)),
                  pl.BlockSpec((None, 2, LANES), lambda b_, p, i: (p, 0, 0))],
        out_specs=pl.BlockSpec((None, bq, LANES), lambda b_, p, i: (b_, i, p)),
        out_shape=jax.ShapeDtypeStruct((b, s, ATT_WIDTH), BF16),
        scratch_shapes=[pltpu.VMEM((s // bk, 2 * bk, 2 * LANES), BF16),
                        pltpu.VMEM((s // bk, 2 * bk, LANES), BF16),
                        pltpu.VMEM((bk, bk), BF16),
                        pltpu.VMEM((bq, LANES), F32),
                        pltpu.VMEM((bq, 2 * LANES), F32),
                        pltpu.VMEM((bq, 2 * bk), F32),
                        pltpu.VMEM((bq, 2 * bk), F32),
                        pltpu.VMEM((bq, 2 * bk), BF16),
                        pltpu.VMEM((bq, 2 * bk), BF16)],
        compiler_params=pltpu.CompilerParams(dimension_semantics=("arbitrary",) * 3, vmem_limit_bytes=VMEM_LIMIT),
        name="attention_prompt",
    )(q, kb, vb, offs)


def _attn_sample_kernel(pt_ref, q_ref, kn_ref, vn_ref, bias_ref, *rest, n_pages):
    del pt_ref
    k_refs = rest[:n_pages]
    v_refs = rest[n_pages:2 * n_pages]
    o_ref = rest[2 * n_pages]
    kn_scr, vn_scr, u_scr = rest[2 * n_pages + 1:]
    dec_seq = q_ref.shape[0]
    rows = dec_seq * N_HEADS

    @pl.when(pl.program_id(0) == 0)
    def _():
        kn_scr[...] = jnp.zeros_like(kn_scr)
        vn_scr[...] = jnp.zeros_like(vn_scr)
        r = lax.broadcasted_iota(jnp.int32, (PAGE_SIZE, PAGE_SIZE), 0)
        c = lax.broadcasted_iota(jnp.int32, (PAGE_SIZE, PAGE_SIZE), 1)
        u_scr[...] = jnp.where(r > c, 1.0, 0.0).astype(BF16)

    kn_scr[0:dec_seq, :] = kn_ref[...]
    vn_scr[0:dec_seq, :] = vn_ref[...]
    bias = bias_ref[...]
    head_of_lane = lax.broadcasted_iota(jnp.int32, (N_HEADS, ATT_WIDTH), 1) >> (HEAD_DIM.bit_length() - 1)
    own_lanes = head_of_lane == lax.broadcasted_iota(jnp.int32, (N_HEADS, ATT_WIDTH), 0)
    qf = q_ref[...].astype(F32)
    q_rows = jnp.concatenate(
        [jnp.where(own_lanes, jnp.broadcast_to(qf[t:t + 1, :], (N_HEADS, ATT_WIDTH)), 0.0) for t in range(dec_seq)],
        axis=0).astype(BF16)

    pages = list(reversed(range(n_pages)))
    contract_last = (((1,), (1,)), ((), ()))
    t_new = lax.broadcasted_iota(jnp.int32, (rows, PAGE_SIZE), 0) >> (N_HEADS.bit_length() - 1)
    new_mask = lax.broadcasted_iota(jnp.int32, (rows, PAGE_SIZE), 1) < t_new

    def page_matrix(ref):
        return ref.reshape(ATT_WIDTH, PAGE_SIZE)[...].astype(BF16)

    nzs = [lax.dot_general(q_rows, kn_scr[...].astype(BF16), contract_last, preferred_element_type=F32) + bias]
    nzs += [jnp.dot(q_rows, page_matrix(k_refs[p]), preferred_element_type=F32) + bias for p in pages]
    lsns = [_log2_one_minus_beta(nz) for nz in nzs]
    lsns[0] = jnp.where(new_mask, lsns[0], 0.0)
    suffix = jnp.dot(jnp.concatenate(lsns, axis=0).astype(BF16), u_scr[...], preferred_element_type=F32)
    carry = jnp.zeros((rows, 1), F32)
    acc = jnp.zeros((rows, ATT_WIDTH), F32)
    for b, (nz, lsn) in enumerate(zip(nzs, lsns)):
        a = jnp.exp2(lsn - nz + suffix[b * rows:(b + 1) * rows] + carry)
        if b == 0:
            a = jnp.where(new_mask, a, 0.0).astype(BF16)
            acc = acc + jnp.dot(a, vn_scr[...].astype(BF16), preferred_element_type=F32)
        else:
            acc = acc + lax.dot_general(a.astype(BF16), page_matrix(v_refs[pages[b - 1]]), contract_last,
                                        preferred_element_type=F32)
        carry = carry + jnp.sum(lsn, axis=1, keepdims=True)
    own = jnp.where(jnp.concatenate([own_lanes] * dec_seq, axis=0), acc, 0.0)
    o_ref[...] = jnp.sum(own.reshape(dec_seq, N_HEADS, ATT_WIDTH), axis=1)


def _attention_sample(q, k_new, v_new, cache_k, cache_v, layer, page_table, bias2):
    db, t, _ = q.shape
    n_pages = page_table.shape[1]
    rows = t * N_HEADS
    bias_rows = jnp.broadcast_to(jnp.tile(bias2, t)[:, None], (rows, PAGE_SIZE))
    new_tokens = lambda: pl.BlockSpec((None, t, ATT_WIDTH), lambda b, pt: (b, 0, 0))
    page = lambda i: pl.BlockSpec((None, None, N_HEADS, HEAD_DIM, PAGE_SIZE),
                                  lambda b, pt, i=i: (layer, pt[b, i], 0, 0, 0))
    grid_spec = pltpu.PrefetchScalarGridSpec(
        num_scalar_prefetch=1,
        grid=(db,),
        in_specs=[new_tokens(), new_tokens(), new_tokens(), pl.BlockSpec((rows, PAGE_SIZE), lambda b, pt: (0, 0))]
                 + [page(i) for i in range(n_pages)] * 2,
        out_specs=new_tokens(),
        scratch_shapes=[pltpu.VMEM((PAGE_SIZE, ATT_WIDTH), F32),
                        pltpu.VMEM((PAGE_SIZE, ATT_WIDTH), F32),
                        pltpu.VMEM((PAGE_SIZE, PAGE_SIZE), BF16)],
    )
    out = pl.pallas_call(
        functools.partial(_attn_sample_kernel, n_pages=n_pages),
        grid_spec=grid_spec,
        out_shape=jax.ShapeDtypeStruct((db, t, ATT_WIDTH), F32),
        compiler_params=pltpu.CompilerParams(dimension_semantics=("arbitrary",), vmem_limit_bytes=VMEM_LIMIT),
        name="attention_sample",
    )(page_table, q, k_new, v_new, bias_rows, *([cache_k] * n_pages), *([cache_v] * n_pages))
    return out.reshape(db * t, ATT_WIDTH).astype(BF16)


def _merge_kernel(att_ref, u_ref, sv_ref, gate_ref, x_ref, woa_ref, wos_ref, wo_ref, wmix_ref, bmix_ref,
                  o_ref, mix_scr, *, chunk_len, tm):
    @pl.when(pl.program_id(0) == 0)
    def _():
        shift = chunk_len.bit_length() - 1
        r = lax.broadcasted_iota(jnp.int32, (CHUNK, CHUNK), 0)
        c = lax.broadcasted_iota(jnp.int32, (CHUNK, CHUNK), 1)
        keep = jnp.logical_and((r >> shift) == (c >> shift), c <= r)
        mix_scr[...] = jnp.where(keep[None], wmix_ref[...], 0.0).astype(BF16)

    first_group = lax.broadcasted_iota(jnp.int32, (1, LANES), 1) < SG_GROUP_DIM
    pieces = []
    for ch in range(tm // CHUNK):
        rows = slice(ch * CHUNK, (ch + 1) * CHUNK)
        svc = sv_ref[rows, :].astype(BF16)
        cols = []
        for p in range(SG_WIDTH // LANES):
            blk = svc[:, p * LANES:(p + 1) * LANES]
            m0 = jnp.dot(mix_scr[2 * p], blk, preferred_element_type=F32)
            m1 = jnp.dot(mix_scr[2 * p + 1], blk, preferred_element_type=F32)
            cols.append(jnp.where(first_group, m0, m1))
        mixed = jnp.concatenate(cols, axis=1) + bmix_ref[...]
        pieces.append(u_ref[rows, :] * mixed)
    sg = jnp.concatenate(pieces, axis=0).astype(BF16)
    merged = (gate_ref[:, :D_MODEL] * jnp.dot(att_ref[...], woa_ref[...], preferred_element_type=F32)
              + gate_ref[:, D_MODEL:] * jnp.dot(sg, wos_ref[...], preferred_element_type=F32))
    o_ref[...] = x_ref[...] + jnp.dot(merged.astype(BF16), wo_ref[...], preferred_element_type=F32)


def _merge(att, u, sv, gates, x, woa_bf, wos_bf, wo_bf, w_sg, b_sg, chunk_len):
    n = x.shape[0]
    tm = min(TOKEN_TILE, n)
    reps = CHUNK // chunk_len
    wmix = jnp.tile(w_sg[:, :chunk_len, :chunk_len], (1, reps, reps))
    bmix = jnp.repeat(jnp.tile(b_sg[:, :chunk_len].T, (reps, 1)), SG_GROUP_DIM, axis=1)
    row = lambda w: pl.BlockSpec((tm, w), lambda i: (i, 0))
    return pl.pallas_call(
        functools.partial(_merge_kernel, chunk_len=chunk_len, tm=tm),
        grid=(n // tm,),
        in_specs=[row(ATT_WIDTH), row(SG_WIDTH), row(SG_WIDTH), row(2 * D_MODEL), row(D_MODEL),
                  _const_spec((ATT_WIDTH, D_MODEL)), _const_spec((SG_WIDTH, D_MODEL)),
                  _const_spec((D_MODEL, D_MODEL)), _const_spec((SG_GROUPS, CHUNK, CHUNK)),
                  _const_spec((CHUNK, SG_WIDTH))],
        out_specs=row(D_MODEL),
        out_shape=jax.ShapeDtypeStruct((n, D_MODEL), F32),
        scratch_shapes=[pltpu.VMEM((SG_GROUPS, CHUNK, CHUNK), BF16)],
        compiler_params=pltpu.CompilerParams(dimension_semantics=("arbitrary",), vmem_limit_bytes=VMEM_LIMIT),
        name="merge",
    )(att, u, sv, gates, x, woa_bf, wos_bf, wo_bf, wmix, bmix)


def _mlp_kernel(x_ref, g_ref, w1_ref, w2_ref, gf_ref, o_ref, *, final):
    x = x_ref[...]
    xn = _rms(x, g_ref[...]).astype(BF16)
    acc = x
    for c in range(D_FF // FF_CHUNK):
        cols = slice(c * FF_CHUNK, (c + 1) * FF_CHUNK)
        h = jnp.maximum(jnp.dot(xn, w1_ref[:, cols], preferred_element_type=F32), 0.0)
        acc = acc + jnp.dot((h * h).astype(BF16), w2_ref[cols, :], preferred_element_type=F32)
    if final:
        acc = _rms(acc, gf_ref[...])
    o_ref[...] = acc


def _mlp(x, g, w1_bf, w2_bf, g_final, final):
    n = x.shape[0]
    tm = min(TOKEN_TILE, n)
    row = pl.BlockSpec((tm, D_MODEL), lambda i: (i, 0))
    return pl.pallas_call(
        functools.partial(_mlp_kernel, final=final),
        grid=(n // tm,),
        in_specs=[row, _const_spec((1, D_MODEL)), _const_spec((D_MODEL, D_FF)), _const_spec((D_FF, D_MODEL)),
                  _const_spec((1, D_MODEL))],
        out_specs=row,
        out_shape=jax.ShapeDtypeStruct((n, D_MODEL), F32),
        compiler_params=pltpu.CompilerParams(dimension_semantics=("arbitrary",), vmem_limit_bytes=VMEM_LIMIT),
        name="mlp",
    )(x, g.reshape(1, -1), w1_bf, w2_bf, g_final.reshape(1, -1))


def kernel(x_prompt, x_sample, cache_k, cache_v, page_table, g_mix, w_in, b_gate, b_sb, g_sv, w_sg, b_sg,
           w_o_att, w_o_sg, w_o, g_ffn, w_ff1, w_ff2, g_final):
    batch, seq, _ = x_prompt.shape
    dec_batch, dec_seq, _ = x_sample.shape
    depth = w_in.shape[0]
    n_pool = cache_k.shape[1]
    xp = x_prompt.reshape(batch * seq, D_MODEL)
    xs = x_sample.reshape(dec_batch * dec_seq, D_MODEL)
    ck = cache_k.transpose(0, 1, 3, 4, 2)
    cv = cache_v.transpose(0, 1, 3, 4, 2)
    kp_l, vp_l, ks_l, vs_l, sgv_l = [], [], [], [], []
    for l in range(depth):
        w_in_bf = w_in[l].astype(BF16)
        woa_bf, wos_bf, wo_bf = w_o_att[l].astype(BF16), w_o_sg[l].astype(BF16), w_o[l].astype(BF16)
        w1_bf, w2_bf = w_ff1[l].astype(BF16), w_ff2[l].astype(BF16)
        bias2 = -b_sb[l] * LOG2E
        final = l == depth - 1

        q, k, v, kb, vb, u, sv, gates = _in_projection(xp, g_mix[l], w_in_bf, b_gate[l], g_sv[l])
        att = _attention_prompt(q.reshape(batch, seq, ATT_WIDTH), kb.reshape(batch, seq, ATT_WIDTH),
                                vb.reshape(batch, seq, ATT_WIDTH), bias2).reshape(batch * seq, ATT_WIDTH)
        xp = _merge(att, u, sv, gates, xp, woa_bf, wos_bf, wo_bf, w_sg[l], b_sg[l], CHUNK)
        kp_l.append(k.reshape(batch, seq, N_HEADS, HEAD_DIM))
        vp_l.append(v.reshape(batch, seq, N_HEADS, HEAD_DIM))

        q, k, v, _, _, u, sv, gates = _in_projection(xs, g_mix[l], w_in_bf, b_gate[l], g_sv[l])
        shp = (dec_batch, dec_seq, ATT_WIDTH)
        att = _attention_sample(q.reshape(shp), k.reshape(shp), v.reshape(shp), ck, cv, l, page_table, bias2)
        xs = _merge(att, u, sv, gates, xs, woa_bf, wos_bf, wo_bf, w_sg[l], b_sg[l], dec_seq)
        ks_l.append(k.reshape(dec_batch, dec_seq, N_HEADS, HEAD_DIM))
        vs_l.append(v.reshape(dec_batch, dec_seq, N_HEADS, HEAD_DIM))
        sgv_l.append(sv.reshape(dec_batch, dec_seq, SG_GROUPS, SG_GROUP_DIM))

        xp = _mlp(xp, g_ffn[l], w1_bf, w2_bf, g_final, final)
        xs = _mlp(xs, g_ffn[l], w1_bf, w2_bf, g_final, final)

    return (xp.reshape(batch, seq, D_MODEL), xs.reshape(dec_batch, dec_seq, D_MODEL),
            jnp.stack(kp_l), jnp.stack(vp_l), jnp.stack(ks_l), jnp.stack(vs_l), jnp.stack(sgv_l))
```

```python
import functools
import math

import jax
import jax.numpy as jnp
from jax import lax
from jax.experimental import pallas as pl
from jax.experimental.pallas import tpu as pltpu

F32 = jnp.float32
BF16 = jnp.bfloat16

D_MODEL = 1024
N_HEADS = 8
HEAD_DIM = 64
ATT_WIDTH = N_HEADS * HEAD_DIM
SG_GROUPS = 8
SG_GROUP_DIM = 64
SG_WIDTH = SG_GROUPS * SG_GROUP_DIM
CHUNK = 128
D_FF = 4 * D_MODEL
PAGE_SIZE = 128
EPS = 1e-6
IN_WIDTH = 3 * ATT_WIDTH + 2 * SG_WIDTH + 2 * D_MODEL

LANES = 128
SUBLANES = 8
LOG2E = math.log2(math.e)
Q_SCALE = -(HEAD_DIM ** -0.5) * LOG2E

TOKEN_TILE = 512
ATT_BQ = 1024
ATT_BK = 256
FF_CHUNK = 1024
BIAS_PIECES = 3
VMEM_LIMIT = 56 * 1024 * 1024


def _rms(x, g):
    return x * lax.rsqrt(jnp.mean(x * x, axis=-1, keepdims=True) + EPS) * g


def _const_spec(shape):
    return pl.BlockSpec(shape, lambda *_: (0,) * len(shape), pipeline_mode=pl.Buffered(1))


def _log2_one_minus_beta(nz):
    return jnp.minimum(nz, 0.0) - jnp.log(1.0 + jnp.exp2(-jnp.abs(nz))) * LOG2E


def _inproj_kernel(x_ref, g_ref, w_ref, bg_ref, gsv_ref,
                   q_ref, k_ref, v_ref, kb_ref, vb_ref, u_ref, sv_ref, gate_ref):
    xn = _rms(x_ref[...], g_ref[...]).astype(BF16)

    def proj(lo, hi):
        return jnp.dot(xn, w_ref[:, lo:hi], preferred_element_type=F32)

    o_k, o_v, o_u, o_sv, o_g = ATT_WIDTH, 2 * ATT_WIDTH, 3 * ATT_WIDTH, 3 * ATT_WIDTH + SG_WIDTH, 3 * ATT_WIDTH + 2 * SG_WIDTH
    q_ref[...] = (proj(0, o_k) * Q_SCALE).astype(BF16)
    k = proj(o_k, o_v)
    k_ref[...] = k
    kb_ref[...] = k.astype(BF16)
    v = proj(o_v, o_u)
    v_ref[...] = v
    vb_ref[...] = v.astype(BF16)
    u_ref[...] = jax.nn.gelu(proj(o_u, o_sv)).astype(u_ref.dtype)
    sv_ref[...] = _rms(jax.nn.gelu(proj(o_sv, o_g)), gsv_ref[...])
    gate_ref[...] = jax.nn.sigmoid(proj(o_g, IN_WIDTH) + bg_ref[...]).astype(gate_ref.dtype)


def _in_projection(x, g, w_bf, b_gate, g_sv):
    n = x.shape[0]
    tm = min(TOKEN_TILE, n)
    row = lambda w: pl.BlockSpec((tm, w), lambda i: (i, 0))
    return pl.pallas_call(
        _inproj_kernel,
        grid=(n // tm,),
        in_specs=[row(D_MODEL), _const_spec((1, D_MODEL)), _const_spec((D_MODEL, IN_WIDTH)),
                  _const_spec((1, 2 * D_MODEL)), _const_spec((1, SG_WIDTH))],
        out_specs=[row(ATT_WIDTH)] * 5 + [row(SG_WIDTH)] * 2 + [row(2 * D_MODEL)],
        out_shape=[jax.ShapeDtypeStruct((n, ATT_WIDTH), BF16),
                   jax.ShapeDtypeStruct((n, ATT_WIDTH), F32),
                   jax.ShapeDtypeStruct((n, ATT_WIDTH), F32),
                   jax.ShapeDtypeStruct((n, ATT_WIDTH), BF16),
                   jax.ShapeDtypeStruct((n, ATT_WIDTH), BF16),
                   jax.ShapeDtypeStruct((n, SG_WIDTH), BF16),
                   jax.ShapeDtypeStruct((n, SG_WIDTH), F32),
                   jax.ShapeDtypeStruct((n, 2 * D_MODEL), BF16)],
        compiler_params=pltpu.CompilerParams(dimension_semantics=("arbitrary",), vmem_limit_bytes=VMEM_LIMIT),
        name="in_projection",
    )(x, g.reshape(1, -1), w_bf, b_gate.reshape(1, -1), g_sv.reshape(1, -1))


def _attn_prompt_kernel(q_ref, k_ref, v_ref, bias_ref, o_ref, kst, vst, u2, o_acc, c_acc,
                        nz_even, nz_odd, a_even, a_odd, *, bq, bk, seq):
    i = pl.program_id(2)
    nkb = seq // bk
    per_tile = bq // bk
    reps = bk // LANES
    nz_bufs = (nz_even, nz_odd)
    a_bufs = (a_even, a_odd)

    @pl.when(i == 0)
    def _():
        lane = lax.broadcasted_iota(jnp.int32, (1, LANES), 1)
        even = jnp.where(lane < HEAD_DIM, 1.0, 0.0).astype(BF16)
        odd = jnp.where(lane < HEAD_DIM, 0.0, 1.0).astype(BF16)
        k3 = k_ref[...].reshape(nkb, bk, LANES)
        kst[:, 0:bk, 0:LANES] = k3 * even
        kst[:, bk:2 * bk, 0:LANES] = k3 * odd
        kst[:, 0:bk, LANES:] = jnp.broadcast_to(bias_ref[0:1, :][None], (nkb, bk, LANES)).astype(BF16)
        kst[:, bk:2 * bk, LANES:] = jnp.broadcast_to(bias_ref[1:2, :][None], (nkb, bk, LANES)).astype(BF16)
        v3 = v_ref[...].reshape(nkb, bk, LANES)
        vst[:, 0:bk, :] = v3 * even
        vst[:, bk:2 * bk, :] = v3 * odd
        r = lax.broadcasted_iota(jnp.int32, (bk, bk), 0)
        c = lax.broadcasted_iota(jnp.int32, (bk, bk), 1)
        u2[...] = jnp.where(r > c, 1.0, 0.0).astype(BF16)

    ones = jnp.where(lax.broadcasted_iota(jnp.int32, (bq, LANES), 1) < BIAS_PIECES, 1.0, 0.0).astype(BF16)
    q_ext = jnp.concatenate([q_ref[...], ones], axis=1)
    o_acc[...] = jnp.zeros_like(o_acc)
    c_acc[...] = jnp.zeros_like(c_acc)

    def scores(j, r0=0):
        return lax.dot_general(q_ext[r0:], kst[j], (((1,), (1,)), ((), ())), preferred_element_type=F32)

    def weights(nz, mask, r0=0):
        n = bq - r0
        lsn = _log2_one_minus_beta(nz)
        if mask is not None:
            lsn = jnp.where(mask, lsn, 0.0)
        log2_beta = lsn - nz
        stacked = jnp.concatenate([lsn[:, :bk], lsn[:, bk:]], axis=0).astype(BF16)
        suf = jnp.dot(stacked, u2[...], preferred_element_type=F32)
        suffix = jnp.concatenate([suf[:n], suf[n:]], axis=1)
        c = c_acc[r0:, :]
        carry = jnp.concatenate([c[:, :LANES]] * reps + [c[:, LANES:]] * reps, axis=1)
        a = jnp.exp2(log2_beta + suffix + carry)
        if mask is not None:
            a = jnp.where(mask, a, 0.0)
        t_even = jnp.sum(lsn[:, :bk], axis=1, keepdims=True)
        t_odd = jnp.sum(lsn[:, bk:], axis=1, keepdims=True)
        c_acc[r0:, :] = c + jnp.concatenate(
            [jnp.broadcast_to(t_even, (n, LANES)), jnp.broadcast_to(t_odd, (n, LANES))], axis=1)
        return a.astype(BF16)

    def accumulate(a, j, r0=0):
        o_acc[r0:, :] += jnp.dot(a, vst[j], preferred_element_type=F32)

    n_full = i * per_tile
    nz = scores(n_full + per_tile - 1, (per_tile - 1) * bk)
    pending = None
    for d in reversed(range(per_tile)):
        r0 = d * bk
        if d > 0:
            nz_next = scores(n_full + d - 1, r0 - bk)
        else:
            nz_next = scores(jnp.maximum(n_full - 1, 0))
        if pending is not None:
            accumulate(*pending)
        row = lax.broadcasted_iota(jnp.int32, (bq - r0, 2 * bk), 0)
        col = lax.broadcasted_iota(jnp.int32, (bq - r0, 2 * bk), 1) & (bk - 1)
        pending = (weights(nz, col < row, r0), n_full + d, r0)
        nz = nz_next

    a_bufs[0][...] = pending[0]
    nz_bufs[0][...] = nz

    def run_blocks(first, count):
        assert count % 2 == 0
        for step in range(count):
            j = first - step
            cur, nxt = step % 2, 1 - step % 2
            nz = nz_bufs[cur][...]
            nz_bufs[nxt][...] = scores(jnp.maximum(j - 1, 0))
            accumulate(a_bufs[cur][...], j + 1)
            a_bufs[nxt][...] = weights(nz, None)

    def body(t, _):
        run_blocks(n_full - 1 - 2 * per_tile * t, 2 * per_tile)
        return 0

    lax.fori_loop(0, lax.shift_right_logical(i, 1), body, 0)

    @pl.when((i & 1) == 1)
    def _():
        run_blocks(per_tile - 1, per_tile)

    accumulate(a_bufs[0][...], 0)
    o_ref[...] = o_acc[...].astype(o_ref.dtype)


def _bf16_pieces(x):
    pieces = []
    for _ in range(BIAS_PIECES):
        p = x.astype(BF16).astype(F32)
        pieces.append(p)
        x = x - p
    return jnp.stack(pieces, axis=-1)


def _attention_prompt(q, kb, vb, bias2):
    b, s, _ = q.shape
    bq, bk = ATT_BQ, ATT_BK
    n_pairs = N_HEADS // 2
    offs = jnp.pad(_bf16_pieces(bias2), ((0, 0), (0, LANES - BIAS_PIECES))).reshape(n_pairs, 2, LANES)
    kern = functools.partial(_attn_prompt_kernel, bq=bq, bk=bk, seq=s)
    return pl.pallas_call(
        kern,
        grid=(b, n_pairs, s // bq),
        in_specs=[pl.BlockSpec((None, bq, LANES), lambda b_, p, i: (b_, i, p)),
                  pl.BlockSpec((None, s, LANES), lambda b_, p, i: (b_, 0, p)),
                  pl.BlockSpec((None, s, LANES), lambda b_, p, i: (b_, 0, p)),
                  pl.BlockSpec((None, 2, LANES), lambda b_, p, i: (p, 0, 0))],
        out_specs=pl.BlockSpec((None, bq, LANES), lambda b_, p, i: (b_, i, p)),
        out_shape=jax.ShapeDtypeStruct((b, s, ATT_WIDTH), BF16),
        scratch_shapes=[pltpu.VMEM((s // bk, 2 * bk, 2 * LANES), BF16),
                        pltpu.VMEM((s // bk, 2 * bk, LANES), BF16),
                        pltpu.VMEM((bk, bk), BF16),
                        pltpu.VMEM((bq, LANES), F32),
                        pltpu.VMEM((bq, 2 * LANES), F32),
                        pltpu.VMEM((bq, 2 * bk), F32),
                        pltpu.VMEM((bq, 2 * bk), F32),
                        pltpu.VMEM((bq, 2 * bk), BF16),
                        pltpu.VMEM((bq, 2 * bk), BF16)],
        compiler_params=pltpu.CompilerParams(dimension_semantics=("arbitrary",) * 3, vmem_limit_bytes=VMEM_LIMIT),
        name="attention_prompt",
    )(q, kb, vb, offs)


def _attn_sample_kernel(pt_ref, q_ref, kn_ref, vn_ref, bias_ref, *rest, n_pages):
    del pt_ref
    k_refs = rest[:n_pages]
    v_refs = rest[n_pages:2 * n_pages]
    o_ref = rest[2 * n_pages]
    kn_scr, vn_scr, u_scr = rest[2 * n_pages + 1:]
    dec_seq = q_ref.shape[0]
    rows = dec_seq * N_HEADS

    @pl.when(pl.program_id(0) == 0)
    def _():
        kn_scr[...] = jnp.zeros_like(kn_scr)
        vn_scr[...] = jnp.zeros_like(vn_scr)
        r = lax.broadcasted_iota(jnp.int32, (PAGE_SIZE, PAGE_SIZE), 0)
        c = lax.broadcasted_iota(jnp.int32, (PAGE_SIZE, PAGE_SIZE), 1)
        u_scr[...] = jnp.where(r > c, 1.0, 0.0).astype(BF16)

    kn_scr[0:dec_seq, :] = kn_ref[...]
    vn_scr[0:dec_seq, :] = vn_ref[...]
    bias = bias_ref[...]
    head_of_lane = lax.broadcasted_iota(jnp.int32, (N_HEADS, ATT_WIDTH), 1) >> (HEAD_DIM.bit_length() - 1)
    own_lanes = head_of_lane == lax.broadcasted_iota(jnp.int32, (N_HEADS, ATT_WIDTH), 0)
    qf = q_ref[...].astype(F32)
    q_rows = jnp.concatenate(
        [jnp.where(own_lanes, jnp.broadcast_to(qf[t:t + 1, :], (N_HEADS, ATT_WIDTH)), 0.0) for t in range(dec_seq)],
        axis=0).astype(BF16)

    pages = list(reversed(range(n_pages)))
    contract_last = (((1,), (1,)), ((), ()))
    t_new = lax.broadcasted_iota(jnp.int32, (rows, PAGE_SIZE), 0) >> (N_HEADS.bit_length() - 1)
    new_mask = lax.broadcasted_iota(jnp.int32, (rows, PAGE_SIZE), 1) < t_new

    def page_matrix(ref):
        return ref.reshape(ATT_WIDTH, PAGE_SIZE)[...].astype(BF16)

    nzs = [lax.dot_general(q_rows, kn_scr[...].astype(BF16), contract_last, preferred_element_type=F32) + bias]
    nzs += [jnp.dot(q_rows, page_matrix(k_refs[p]), preferred_element_type=F32) + bias for p in pages]
    lsns = [_log2_one_minus_beta(nz) for nz in nzs]
    lsns[0] = jnp.where(new_mask, lsns[0], 0.0)
    suffix = jnp.dot(jnp.concatenate(lsns, axis=0).astype(BF16), u_scr[...], preferred_element_type=F32)
    carry = jnp.zeros((rows, 1), F32)
    acc = jnp.zeros((rows, ATT_WIDTH), F32)
    for b, (nz, lsn) in enumerate(zip(nzs, lsns)):
        a = jnp.exp2(lsn - nz + suffix[b * rows:(b + 1) * rows] + carry)
        if b == 0:
            a = jnp.where(new_mask, a, 0.0).astype(BF16)
            acc = acc + jnp.dot(a, vn_scr[...].astype(BF16), preferred_element_type=F32)
        else:
            acc = acc + lax.dot_general(a.astype(BF16), page_matrix(v_refs[pages[b - 1]]), contract_last,
                                        preferred_element_type=F32)
        carry = carry + jnp.sum(lsn, axis=1, keepdims=True)
    own = jnp.where(jnp.concatenate([own_lanes] * dec_seq, axis=0), acc, 0.0)
    o_ref[...] = jnp.sum(own.reshape(dec_seq, N_HEADS, ATT_WIDTH), axis=1)


def _attention_sample(q, k_new, v_new, cache_k, cache_v, layer, page_table, bias2):
    db, t, _ = q.shape
    n_pages = page_table.shape[1]
    rows = t * N_HEADS
    bias_rows = jnp.broadcast_to(jnp.tile(bias2, t)[:, None], (rows, PAGE_SIZE))
    new_tokens = lambda: pl.BlockSpec((None, t, ATT_WIDTH), lambda b, pt: (b, 0, 0))
    page = lambda i: pl.BlockSpec((None, None, N_HEADS, HEAD_DIM, PAGE_SIZE),
                                  lambda b, pt, i=i: (layer, pt[b, i], 0, 0, 0))
    grid_spec = pltpu.PrefetchScalarGridSpec(
        num_scalar_prefetch=1,
        grid=(db,),
        in_specs=[new_tokens(), new_tokens(), new_tokens(), pl.BlockSpec((rows, PAGE_SIZE), lambda b, pt: (0, 0))]
                 + [page(i) for i in range(n_pages)] * 2,
        out_specs=new_tokens(),
        scratch_shapes=[pltpu.VMEM((PAGE_SIZE, ATT_WIDTH), F32),
                        pltpu.VMEM((PAGE_SIZE, ATT_WIDTH), F32),
                        pltpu.VMEM((PAGE_SIZE, PAGE_SIZE), BF16)],
    )
    out = pl.pallas_call(
        functools.partial(_attn_sample_kernel, n_pages=n_pages),
        grid_spec=grid_spec,
        out_shape=jax.ShapeDtypeStruct((db, t, ATT_WIDTH), F32),
        compiler_params=pltpu.CompilerParams(dimension_semantics=("arbitrary",), vmem_limit_bytes=VMEM_LIMIT),
        name="attention_sample",
    )(page_table, q, k_new, v_new, bias_rows, *([cache_k] * n_pages), *([cache_v] * n_pages))
    return out.reshape(db * t, ATT_WIDTH).astype(BF16)


def _merge_kernel(att_ref, u_ref, sv_ref, gate_ref, x_ref, woa_ref, wos_ref, wo_ref, wmix_ref, bmix_ref,
                  o_ref, mix_scr, *, chunk_len, tm):
    @pl.when(pl.program_id(0) == 0)
    def _():
        shift = chunk_len.bit_length() - 1
        r = lax.broadcasted_iota(jnp.int32, (CHUNK, CHUNK), 0)
        c = lax.broadcasted_iota(jnp.int32, (CHUNK, CHUNK), 1)
        keep = jnp.logical_and((r >> shift) == (c >> shift), c <= r)
        mix_scr[...] = jnp.where(keep[None], wmix_ref[...], 0.0).astype(BF16)

    first_group = lax.broadcasted_iota(jnp.int32, (1, LANES), 1) < SG_GROUP_DIM
    pieces = []
    for ch in range(tm // CHUNK):
        rows = slice(ch * CHUNK, (ch + 1) * CHUNK)
        svc = sv_ref[rows, :].astype(BF16)
        cols = []
        for p in range(SG_WIDTH // LANES):
            blk = svc[:, p * LANES:(p + 1) * LANES]
            m0 = jnp.dot(mix_scr[2 * p], blk, preferred_element_type=F32)
            m1 = jnp.dot(mix_scr[2 * p + 1], blk, preferred_element_type=F32)
            cols.append(jnp.where(first_group, m0, m1))
        mixed = jnp.concatenate(cols, axis=1) + bmix_ref[...]
        pieces.append(u_ref[rows, :].astype(F32) * mixed)
    sg = jnp.concatenate(pieces, axis=0).astype(BF16)
    merged = (gate_ref[:, :D_MODEL].astype(F32) * jnp.dot(att_ref[...], woa_ref[...], preferred_element_type=F32)
              + gate_ref[:, D_MODEL:].astype(F32) * jnp.dot(sg, wos_ref[...], preferred_element_type=F32))
    o_ref[...] = x_ref[...] + jnp.dot(merged.astype(BF16), wo_ref[...], preferred_element_type=F32)


def _merge(att, u, sv, gates, x, woa_bf, wos_bf, wo_bf, w_sg, b_sg, chunk_len):
    n = x.shape[0]
    tm = min(TOKEN_TILE, n)
    reps = CHUNK // chunk_len
    wmix = jnp.tile(w_sg[:, :chunk_len, :chunk_len], (1, reps, reps))
    bmix = jnp.repeat(jnp.tile(b_sg[:, :chunk_len].T, (reps, 1)), SG_GROUP_DIM, axis=1)
    row = lambda w: pl.BlockSpec((tm, w), lambda i: (i, 0))
    return pl.pallas_call(
        functools.partial(_merge_kernel, chunk_len=chunk_len, tm=tm),
        grid=(n // tm,),
        in_specs=[row(ATT_WIDTH), row(SG_WIDTH), row(SG_WIDTH), row(2 * D_MODEL), row(D_MODEL),
                  _const_spec((ATT_WIDTH, D_MODEL)), _const_spec((SG_WIDTH, D_MODEL)),
                  _const_spec((D_MODEL, D_MODEL)), _const_spec((SG_GROUPS, CHUNK, CHUNK)),
                  _const_spec((CHUNK, SG_WIDTH))],
        out_specs=row(D_MODEL),
        out_shape=jax.ShapeDtypeStruct((n, D_MODEL), F32),
        scratch_shapes=[pltpu.VMEM((SG_GROUPS, CHUNK, CHUNK), BF16)],
        compiler_params=pltpu.CompilerParams(dimension_semantics=("arbitrary",), vmem_limit_bytes=VMEM_LIMIT),
        name="merge",
    )(att, u, sv, gates, x, woa_bf, wos_bf, wo_bf, wmix, bmix)


def _mlp_kernel(x_ref, g_ref, w1_ref, w2_ref, gf_ref, o_ref, *, final):
    x = x_ref[...]
    xn = _rms(x, g_ref[...]).astype(BF16)
    acc = x
    for c in range(D_FF // FF_CHUNK):
        cols = slice(c * FF_CHUNK, (c + 1) * FF_CHUNK)
        h = jnp.maximum(jnp.dot(xn, w1_ref[:, cols], preferred_element_type=F32), 0.0)
        acc = acc + jnp.dot((h * h).astype(BF16), w2_ref[cols, :], preferred_element_type=F32)
    if final:
        acc = _rms(acc, gf_ref[...])
    o_ref[...] = acc


def _mlp(x, g, w1_bf, w2_bf, g_final, final):
    n = x.shape[0]
    tm = min(TOKEN_TILE, n)
    row = pl.BlockSpec((tm, D_MODEL), lambda i: (i, 0))
    return pl.pallas_call(
        functools.partial(_mlp_kernel, final=final),
        grid=(n // tm,),
        in_specs=[row, _const_spec((1, D_MODEL)), _const_spec((D_MODEL, D_FF)), _const_spec((D_FF, D_MODEL)),
                  _const_spec((1, D_MODEL))],
        out_specs=row,
        out_shape=jax.ShapeDtypeStruct((n, D_MODEL), F32),
        compiler_params=pltpu.CompilerParams(dimension_semantics=("arbitrary",), vmem_limit_bytes=VMEM_LIMIT),
        name="mlp",
    )(x, g.reshape(1, -1), w1_bf, w2_bf, g_final.reshape(1, -1))


def kernel(x_prompt, x_sample, cache_k, cache_v, page_table, g_mix, w_in, b_gate, b_sb, g_sv, w_sg, b_sg,
           w_o_att, w_o_sg, w_o, g_ffn, w_ff1, w_ff2, g_final):
    batch, seq, _ = x_prompt.shape
    dec_batch, dec_seq, _ = x_sample.shape
    depth = w_in.shape[0]
    n_pool = cache_k.shape[1]
    xp = x_prompt.reshape(batch * seq, D_MODEL)
    xs = x_sample.reshape(dec_batch * dec_seq, D_MODEL)
    ck = cache_k.transpose(0, 1, 3, 4, 2)
    cv = cache_v.transpose(0, 1, 3, 4, 2)
    kp_l, vp_l, ks_l, vs_l, sgv_l = [], [], [], [], []
    for l in range(depth):
        w_in_bf = w_in[l].astype(BF16)
        woa_bf, wos_bf, wo_bf = w_o_att[l].astype(BF16), w_o_sg[l].astype(BF16), w_o[l].astype(BF16)
        w1_bf, w2_bf = w_ff1[l].astype(BF16), w_ff2[l].astype(BF16)
        bias2 = -b_sb[l] * LOG2E
        final = l == depth - 1

        q, k, v, kb, vb, u, sv, gates = _in_projection(xp, g_mix[l], w_in_bf, b_gate[l], g_sv[l])
        att = _attention_prompt(q.reshape(batch, seq, ATT_WIDTH), kb.reshape(batch, seq, ATT_WIDTH),
                                vb.reshape(batch, seq, ATT_WIDTH), bias2).reshape(batch * seq, ATT_WIDTH)
        xp = _merge(att, u, sv, gates, xp, woa_bf, wos_bf, wo_bf, w_sg[l], b_sg[l], CHUNK)
        kp_l.append(k.reshape(batch, seq, N_HEADS, HEAD_DIM))
        vp_l.append(v.reshape(batch, seq, N_HEADS, HEAD_DIM))

        q, k, v, _, _, u, sv, gates = _in_projection(xs, g_mix[l], w_in_bf, b_gate[l], g_sv[l])
        shp = (dec_batch, dec_seq, ATT_WIDTH)
        att = _attention_sample(q.reshape(shp), k.reshape(shp), v.reshape(shp), ck, cv, l, page_table, bias2)
        xs = _merge(att, u, sv, gates, xs, woa_bf, wos_bf, wo_bf, w_sg[l], b_sg[l], dec_seq)
        ks_l.append(k.reshape(dec_batch, dec_seq, N_HEADS, HEAD_DIM))
        vs_l.append(v.reshape(dec_batch, dec_seq, N_HEADS, HEAD_DIM))
        sgv_l.append(sv.reshape(dec_batch, dec_seq, SG_GROUPS, SG_GROUP_DIM))

        xp = _mlp(xp, g_ffn[l], w1_bf, w2_bf, g_final, final)
        xs = _mlp(xs, g_ffn[l], w1_bf, w2_bf, g_final, final)

    return (xp.reshape(batch, seq, D_MODEL), xs.reshape(dec_batch, dec_seq, D_MODEL),
            jnp.stack(kp_l), jnp.stack(vp_l), jnp.stack(ks_l), jnp.stack(vs_l), jnp.stack(sgv_l))
```

```python
import functools
import math

import jax
import jax.numpy as jnp
from jax import lax
from jax.experimental import pallas as pl
from jax.experimental.pallas import tpu as pltpu

F32 = jnp.float32
BF16 = jnp.bfloat16

D_MODEL = 1024
N_HEADS = 8
HEAD_DIM = 64
ATT_WIDTH = N_HEADS * HEAD_DIM
SG_GROUPS = 8
SG_GROUP_DIM = 64
SG_WIDTH = SG_GROUPS * SG_GROUP_DIM
CHUNK = 128
D_FF = 4 * D_MODEL
PAGE_SIZE = 128
EPS = 1e-6
IN_WIDTH = 3 * ATT_WIDTH + 2 * SG_WIDTH + 2 * D_MODEL

LANES = 128
SUBLANES = 8
LOG2E = math.log2(math.e)
Q_SCALE = -(HEAD_DIM ** -0.5) * LOG2E

TOKEN_TILE = 512
ATT_BQ = 1024
ATT_BK = 256
FF_CHUNK = 1024
BIAS_PIECES = 3
VMEM_LIMIT = 56 * 1024 * 1024


def _rms(x, g):
    return x * lax.rsqrt(jnp.mean(x * x, axis=-1, keepdims=True) + EPS) * g


def _const_spec(shape):
    return pl.BlockSpec(shape, lambda *_: (0,) * len(shape), pipeline_mode=pl.Buffered(1))


def _log2_one_minus_beta(nz):
    return jnp.minimum(nz, 0.0) - jnp.log(1.0 + jnp.exp2(-jnp.abs(nz))) * LOG2E


def _inproj_kernel(x_ref, g_ref, w_ref, bg_ref, gsv_ref, *rest):
    q_ref, k_ref, v_ref, kb_ref, vb_ref, u_ref, sv_ref, gate_ref = rest[-8:]
    xn = _rms(x_ref[...], g_ref[...]).astype(BF16)

    def proj(lo, hi):
        return jnp.dot(xn, w_ref[:, lo:hi], preferred_element_type=F32)

    o_k, o_v, o_u, o_sv, o_g = ATT_WIDTH, 2 * ATT_WIDTH, 3 * ATT_WIDTH, 3 * ATT_WIDTH + SG_WIDTH, 3 * ATT_WIDTH + 2 * SG_WIDTH
    q_ref[...] = (proj(0, o_k) * Q_SCALE).astype(BF16)
    k = proj(o_k, o_v)
    k_ref[...] = k
    kb_ref[...] = k.astype(BF16)
    v = proj(o_v, o_u)
    v_ref[...] = v
    vb_ref[...] = v.astype(BF16)
    u_ref[...] = jax.nn.gelu(proj(o_u, o_sv)).astype(u_ref.dtype)
    sv_ref[...] = _rms(jax.nn.gelu(proj(o_sv, o_g)), gsv_ref[...])
    gate_ref[...] = jax.nn.sigmoid(proj(o_g, IN_WIDTH) + bg_ref[...]).astype(gate_ref.dtype)


def _in_projection(x, g, w_bf, b_gate, g_sv, layer=0, depth=1, kv_all=None):
    n = x.shape[0]
    tm = min(TOKEN_TILE, n)
    row = lambda w: pl.BlockSpec((tm, w), lambda i: (i, 0))
    slab = pl.BlockSpec((None, tm, ATT_WIDTH), lambda i: (layer, i, 0))
    in_specs = [row(D_MODEL), _const_spec((1, D_MODEL)), _const_spec((D_MODEL, IN_WIDTH)),
                _const_spec((1, 2 * D_MODEL)), _const_spec((1, SG_WIDTH))]
    args = [x, g.reshape(1, -1), w_bf, b_gate.reshape(1, -1), g_sv.reshape(1, -1)]
    aliases = {}
    if kv_all is not None:
        aliases = {len(args): 1, len(args) + 1: 2}
        in_specs += [pl.BlockSpec(memory_space=pl.ANY)] * 2
        args += list(kv_all)
    return pl.pallas_call(
        _inproj_kernel,
        grid=(n // tm,),
        in_specs=in_specs,
        out_specs=[row(ATT_WIDTH), slab, slab] + [row(ATT_WIDTH)] * 2 + [row(SG_WIDTH)] * 2 + [row(2 * D_MODEL)],
        out_shape=[jax.ShapeDtypeStruct((n, ATT_WIDTH), BF16),
                   jax.ShapeDtypeStruct((depth, n, ATT_WIDTH), F32),
                   jax.ShapeDtypeStruct((depth, n, ATT_WIDTH), F32),
                   jax.ShapeDtypeStruct((n, ATT_WIDTH), BF16),
                   jax.ShapeDtypeStruct((n, ATT_WIDTH), BF16),
                   jax.ShapeDtypeStruct((n, SG_WIDTH), BF16),
                   jax.ShapeDtypeStruct((n, SG_WIDTH), F32),
                   jax.ShapeDtypeStruct((n, 2 * D_MODEL), BF16)],
        input_output_aliases=aliases,
        compiler_params=pltpu.CompilerParams(dimension_semantics=("arbitrary",), vmem_limit_bytes=VMEM_LIMIT),
        name="in_projection",
    )(*args)


def _attn_prompt_kernel(q_ref, k_ref, v_ref, bias_ref, o_ref, kst, vst, u2, o_acc, c_acc,
                        nz_even, nz_odd, a_even, a_odd, *, bq, bk, seq):
    i = pl.program_id(2)
    nkb = seq // bk
    per_tile = bq // bk
    reps = bk // LANES
    nz_bufs = (nz_even, nz_odd)
    a_bufs = (a_even, a_odd)

    @pl.when(i == 0)
    def _():
        lane = lax.broadcasted_iota(jnp.int32, (1, LANES), 1)
        even = jnp.where(lane < HEAD_DIM, 1.0, 0.0).astype(BF16)
        odd = jnp.where(lane < HEAD_DIM, 0.0, 1.0).astype(BF16)
        k3 = k_ref[...].reshape(nkb, bk, LANES)
        kst[:, 0:bk, 0:LANES] = k3 * even
        kst[:, bk:2 * bk, 0:LANES] = k3 * odd
        kst[:, 0:bk, LANES:] = jnp.broadcast_to(bias_ref[0:1, :][None], (nkb, bk, LANES)).astype(BF16)
        kst[:, bk:2 * bk, LANES:] = jnp.broadcast_to(bias_ref[1:2, :][None], (nkb, bk, LANES)).astype(BF16)
        v3 = v_ref[...].reshape(nkb, bk, LANES)
        vst[:, 0:bk, :] = v3 * even
        vst[:, bk:2 * bk, :] = v3 * odd
        r = lax.broadcasted_iota(jnp.int32, (bk, bk), 0)
        c = lax.broadcasted_iota(jnp.int32, (bk, bk), 1)
        u2[...] = jnp.where(r > c, 1.0, 0.0).astype(BF16)

    ones = jnp.where(lax.broadcasted_iota(jnp.int32, (bq, LANES), 1) < BIAS_PIECES, 1.0, 0.0).astype(BF16)
    q_ext = jnp.concatenate([q_ref[...], ones], axis=1)
    o_acc[...] = jnp.zeros_like(o_acc)
    c_acc[...] = jnp.zeros_like(c_acc)

    def scores(j, r0=0):
        return lax.dot_general(q_ext[r0:], kst[j], (((1,), (1,)), ((), ())), preferred_element_type=F32)

    def weights(nz, mask, r0=0):
        n = bq - r0
        lsn = _log2_one_minus_beta(nz)
        if mask is not None:
            lsn = jnp.where(mask, lsn, 0.0)
        log2_beta = lsn - nz
        stacked = jnp.concatenate([lsn[:, :bk], lsn[:, bk:]], axis=0).astype(BF16)
        suf = jnp.dot(stacked, u2[...], preferred_element_type=F32)
        suffix = jnp.concatenate([suf[:n], suf[n:]], axis=1)
        c = c_acc[r0:, :]
        carry = jnp.concatenate([c[:, :LANES]] * reps + [c[:, LANES:]] * reps, axis=1)
        a = jnp.exp2(log2_beta + suffix + carry)
        if mask is not None:
            a = jnp.where(mask, a, 0.0)
        t_even = jnp.sum(lsn[:, :bk], axis=1, keepdims=True)
        t_odd = jnp.sum(lsn[:, bk:], axis=1, keepdims=True)
        c_acc[r0:, :] = c + jnp.concatenate(
            [jnp.broadcast_to(t_even, (n, LANES)), jnp.broadcast_to(t_odd, (n, LANES))], axis=1)
        return a.astype(BF16)

    def accumulate(a, j, r0=0):
        o_acc[r0:, :] += jnp.dot(a, vst[j], preferred_element_type=F32)

    n_full = i * per_tile
    nz = scores(n_full + per_tile - 1, (per_tile - 1) * bk)
    pending = None
    for d in reversed(range(per_tile)):
        r0 = d * bk
        if d > 0:
            nz_next = scores(n_full + d - 1, r0 - bk)
        else:
            nz_next = scores(jnp.maximum(n_full - 1, 0))
        if pending is not None:
            accumulate(*pending)
        row = lax.broadcasted_iota(jnp.int32, (bq - r0, 2 * bk), 0)
        col = lax.broadcasted_iota(jnp.int32, (bq - r0, 2 * bk), 1) & (bk - 1)
        pending = (weights(nz, col < row, r0), n_full + d, r0)
        nz = nz_next

    a_bufs[0][...] = pending[0]
    nz_bufs[0][...] = nz

    def run_blocks(first, count):
        assert count % 2 == 0
        for step in range(count):
            j = first - step
            cur, nxt = step % 2, 1 - step % 2
            nz = nz_bufs[cur][...]
            nz_bufs[nxt][...] = scores(jnp.maximum(j - 1, 0))
            accumulate(a_bufs[cur][...], j + 1)
            a_bufs[nxt][...] = weights(nz, None)

    def body(t, _):
        run_blocks(n_full - 1 - 2 * per_tile * t, 2 * per_tile)
        return 0

    lax.fori_loop(0, lax.shift_right_logical(i, 1), body, 0)

    @pl.when((i & 1) == 1)
    def _():
        run_blocks(per_tile - 1, per_tile)

    accumulate(a_bufs[0][...], 0)
    o_ref[...] = o_acc[...].astype(o_ref.dtype)


def _bf16_pieces(x):
    pieces = []
    for _ in range(BIAS_PIECES):
        p = x.astype(BF16).astype(F32)
        pieces.append(p)
        x = x - p
    return jnp.stack(pieces, axis=-1)


def _attention_prompt(q, kb, vb, bias2):
    b, s, _ = q.shape
    bq, bk = ATT_BQ, ATT_BK
    n_pairs = N_HEADS // 2
    offs = jnp.pad(_bf16_pieces(bias2), ((0, 0), (0, LANES - BIAS_PIECES))).reshape(n_pairs, 2, LANES)
    kern = functools.partial(_attn_prompt_kernel, bq=bq, bk=bk, seq=s)
    return pl.pallas_call(
        kern,
        grid=(b, n_pairs, s // bq),
        in_specs=[pl.BlockSpec((None, bq, LANES), lambda b_, p, i: (b_, i, p)),
                  pl.BlockSpec((None, s, LANES), lambda b_, p, i: (b_, 0, p)),
                  pl.BlockSpec((None, s, LANES), lambda b_, p, i: (b_, 0, p)),
                  pl.BlockSpec((None, 2, LANES), lambda b_, p, i: (p, 0, 0))],
        out_specs=pl.BlockSpec((None, bq, LANES), lambda b_, p, i: (b_, i, p)),
        out_shape=jax.ShapeDtypeStruct((b, s, ATT_WIDTH), BF16),
        scratch_shapes=[pltpu.VMEM((s // bk, 2 * bk, 2 * LANES), BF16),
                        pltpu.VMEM((s // bk, 2 * bk, LANES), BF16),
                        pltpu.VMEM((bk, bk), BF16),
                        pltpu.VMEM((bq, LANES), F32),
                        pltpu.VMEM((bq, 2 * LANES), F32),
                        pltpu.VMEM((bq, 2 * bk), F32),
                        pltpu.VMEM((bq, 2 * bk), F32),
                        pltpu.VMEM((bq, 2 * bk), BF16),
                        pltpu.VMEM((bq, 2 * bk), BF16)],
        compiler_params=pltpu.CompilerParams(dimension_semantics=("arbitrary",) * 3, vmem_limit_bytes=VMEM_LIMIT),
        name="attention_prompt",
    )(q, kb, vb, offs)


def _attn_sample_kernel(pt_ref, q_ref, kn_ref, vn_ref, bias_ref, *rest, n_pages):
    del pt_ref
    k_refs = rest[:n_pages]
    v_refs = rest[n_pages:2 * n_pages]
    o_ref = rest[2 * n_pages]
    kn_scr, vn_scr, u_scr = rest[2 * n_pages + 1:]
    dec_seq = q_ref.shape[0]
    rows = dec_seq * N_HEADS

    @pl.when(pl.program_id(0) == 0)
    def _():
        kn_scr[...] = jnp.zeros_like(kn_scr)
        vn_scr[...] = jnp.zeros_like(vn_scr)
        r = lax.broadcasted_iota(jnp.int32, (PAGE_SIZE, PAGE_SIZE), 0)
        c = lax.broadcasted_iota(jnp.int32, (PAGE_SIZE, PAGE_SIZE), 1)
        u_scr[...] = jnp.where(r > c, 1.0, 0.0).astype(BF16)

    kn_scr[0:dec_seq, :] = kn_ref[...]
    vn_scr[0:dec_seq, :] = vn_ref[...]
    bias = bias_ref[...]
    head_of_lane = lax.broadcasted_iota(jnp.int32, (N_HEADS, ATT_WIDTH), 1) >> (HEAD_DIM.bit_length() - 1)
    own_lanes = head_of_lane == lax.broadcasted_iota(jnp.int32, (N_HEADS, ATT_WIDTH), 0)
    qf = q_ref[...].astype(F32)
    q_rows = jnp.concatenate(
        [jnp.where(own_lanes, jnp.broadcast_to(qf[t:t + 1, :], (N_HEADS, ATT_WIDTH)), 0.0) for t in range(dec_seq)],
        axis=0).astype(BF16)

    pages = list(reversed(range(n_pages)))
    contract_last = (((1,), (1,)), ((), ()))
    t_new = lax.broadcasted_iota(jnp.int32, (rows, PAGE_SIZE), 0) >> (N_HEADS.bit_length() - 1)
    new_mask = lax.broadcasted_iota(jnp.int32, (rows, PAGE_SIZE), 1) < t_new

    def page_matrix(ref):
        return ref.reshape(ATT_WIDTH, PAGE_SIZE)[...].astype(BF16)

    nzs = [lax.dot_general(q_rows, kn_scr[...].astype(BF16), contract_last, preferred_element_type=F32) + bias]
    nzs += [jnp.dot(q_rows, page_matrix(k_refs[p]), preferred_element_type=F32) + bias for p in pages]
    lsns = [_log2_one_minus_beta(nz) for nz in nzs]
    lsns[0] = jnp.where(new_mask, lsns[0], 0.0)
    suffix = jnp.dot(jnp.concatenate(lsns, axis=0).astype(BF16), u_scr[...], preferred_element_type=F32)
    carry = jnp.zeros((rows, 1), F32)
    acc = jnp.zeros((rows, ATT_WIDTH), F32)
    for b, (nz, lsn) in enumerate(zip(nzs, lsns)):
        a = jnp.exp2(lsn - nz + suffix[b * rows:(b + 1) * rows] + carry)
        if b == 0:
            a = jnp.where(new_mask, a, 0.0).astype(BF16)
            acc = acc + jnp.dot(a, vn_scr[...].astype(BF16), preferred_element_type=F32)
        else:
            acc = acc + lax.dot_general(a.astype(BF16), page_matrix(v_refs[pages[b - 1]]), contract_last,
                                        preferred_element_type=F32)
        carry = carry + jnp.sum(lsn, axis=1, keepdims=True)
    own = jnp.where(jnp.concatenate([own_lanes] * dec_seq, axis=0), acc, 0.0)
    o_ref[...] = jnp.sum(own.reshape(dec_seq, N_HEADS, ATT_WIDTH), axis=1)


def _attention_sample(q, k_new, v_new, cache_k, cache_v, layer, page_table, bias2):
    db, t, _ = q.shape
    n_pages = page_table.shape[1]
    rows = t * N_HEADS
    bias_rows = jnp.broadcast_to(jnp.tile(bias2, t)[:, None], (rows, PAGE_SIZE))
    new_tokens = lambda: pl.BlockSpec((None, t, ATT_WIDTH), lambda b, pt: (b, 0, 0))
    page = lambda i: pl.BlockSpec((None, None, N_HEADS, HEAD_DIM, PAGE_SIZE),
                                  lambda b, pt, i=i: (layer, pt[b, i], 0, 0, 0))
    grid_spec = pltpu.PrefetchScalarGridSpec(
        num_scalar_prefetch=1,
        grid=(db,),
        in_specs=[new_tokens(), new_tokens(), new_tokens(), pl.BlockSpec((rows, PAGE_SIZE), lambda b, pt: (0, 0))]
                 + [page(i) for i in range(n_pages)] * 2,
        out_specs=new_tokens(),
        scratch_shapes=[pltpu.VMEM((PAGE_SIZE, ATT_WIDTH), F32),
                        pltpu.VMEM((PAGE_SIZE, ATT_WIDTH), F32),
                        pltpu.VMEM((PAGE_SIZE, PAGE_SIZE), BF16)],
    )
    out = pl.pallas_call(
        functools.partial(_attn_sample_kernel, n_pages=n_pages),
        grid_spec=grid_spec,
        out_shape=jax.ShapeDtypeStruct((db, t, ATT_WIDTH), F32),
        compiler_params=pltpu.CompilerParams(dimension_semantics=("arbitrary",), vmem_limit_bytes=VMEM_LIMIT),
        name="attention_sample",
    )(page_table, q, k_new, v_new, bias_rows, *([cache_k] * n_pages), *([cache_v] * n_pages))
    return out.reshape(db * t, ATT_WIDTH).astype(BF16)


def _merge_kernel(att_ref, u_ref, sv_ref, gate_ref, x_ref, woa_ref, wos_ref, wo_ref, wmix_ref, bmix_ref,
                  o_ref, mix_scr, *, chunk_len, tm):
    @pl.when(pl.program_id(0) == 0)
    def _():
        shift = chunk_len.bit_length() - 1
        r = lax.broadcasted_iota(jnp.int32, (CHUNK, CHUNK), 0)
        c = lax.broadcasted_iota(jnp.int32, (CHUNK, CHUNK), 1)
        keep = jnp.logical_and((r >> shift) == (c >> shift), c <= r)
        mix_scr[...] = jnp.where(keep[None], wmix_ref[...], 0.0).astype(BF16)

    first_group = lax.broadcasted_iota(jnp.int32, (1, LANES), 1) < SG_GROUP_DIM
    pieces = []
    for ch in range(tm // CHUNK):
        rows = slice(ch * CHUNK, (ch + 1) * CHUNK)
        svc = sv_ref[rows, :].astype(BF16)
        cols = []
        for p in range(SG_WIDTH // LANES):
            blk = svc[:, p * LANES:(p + 1) * LANES]
            m0 = jnp.dot(mix_scr[2 * p], blk, preferred_element_type=F32)
            m1 = jnp.dot(mix_scr[2 * p + 1], blk, preferred_element_type=F32)
            cols.append(jnp.where(first_group, m0, m1))
        mixed = jnp.concatenate(cols, axis=1) + bmix_ref[...]
        pieces.append(u_ref[rows, :].astype(F32) * mixed)
    sg = jnp.concatenate(pieces, axis=0).astype(BF16)
    merged = (gate_ref[:, :D_MODEL].astype(F32) * jnp.dot(att_ref[...], woa_ref[...], preferred_element_type=F32)
              + gate_ref[:, D_MODEL:].astype(F32) * jnp.dot(sg, wos_ref[...], preferred_element_type=F32))
    o_ref[...] = x_ref[...] + jnp.dot(merged.astype(BF16), wo_ref[...], preferred_element_type=F32)


def _merge(att, u, sv, gates, x, woa_bf, wos_bf, wo_bf, w_sg, b_sg, chunk_len):
    n = x.shape[0]
    tm = min(TOKEN_TILE, n)
    reps = CHUNK // chunk_len
    wmix = jnp.tile(w_sg[:, :chunk_len, :chunk_len], (1, reps, reps))
    bmix = jnp.repeat(jnp.tile(b_sg[:, :chunk_len].T, (reps, 1)), SG_GROUP_DIM, axis=1)
    row = lambda w: pl.BlockSpec((tm, w), lambda i: (i, 0))
    return pl.pallas_call(
        functools.partial(_merge_kernel, chunk_len=chunk_len, tm=tm),
        grid=(n // tm,),
        in_specs=[row(ATT_WIDTH), row(SG_WIDTH), row(SG_WIDTH), row(2 * D_MODEL), row(D_MODEL),
                  _const_spec((ATT_WIDTH, D_MODEL)), _const_spec((SG_WIDTH, D_MODEL)),
                  _const_spec((D_MODEL, D_MODEL)), _const_spec((SG_GROUPS, CHUNK, CHUNK)),
                  _const_spec((CHUNK, SG_WIDTH))],
        out_specs=row(D_MODEL),
        out_shape=jax.ShapeDtypeStruct((n, D_MODEL), F32),
        scratch_shapes=[pltpu.VMEM((SG_GROUPS, CHUNK, CHUNK), BF16)],
        compiler_params=pltpu.CompilerParams(dimension_semantics=("arbitrary",), vmem_limit_bytes=VMEM_LIMIT),
        name="merge",
    )(att, u, sv, gates, x, woa_bf, wos_bf, wo_bf, wmix, bmix)


def _mlp_kernel(x_ref, g_ref, w1_ref, w2_ref, gf_ref, o_ref, *, final):
    x = x_ref[...]
    xn = _rms(x, g_ref[...]).astype(BF16)
    acc = x
    for c in range(D_FF // FF_CHUNK):
        cols = slice(c * FF_CHUNK, (c + 1) * FF_CHUNK)
        h = jnp.maximum(jnp.dot(xn, w1_ref[:, cols], preferred_element_type=F32), 0.0)
        acc = acc + jnp.dot((h * h).astype(BF16), w2_ref[cols, :], preferred_element_type=F32)
    if final:
        acc = _rms(acc, gf_ref[...])
    o_ref[...] = acc


def _mlp(x, g, w1_bf, w2_bf, g_final, final):
    n = x.shape[0]
    tm = min(TOKEN_TILE, n)
    row = pl.BlockSpec((tm, D_MODEL), lambda i: (i, 0))
    return pl.pallas_call(
        functools.partial(_mlp_kernel, final=final),
        grid=(n // tm,),
        in_specs=[row, _const_spec((1, D_MODEL)), _const_spec((D_MODEL, D_FF)), _const_spec((D_FF, D_MODEL)),
                  _const_spec((1, D_MODEL))],
        out_specs=row,
        out_shape=jax.ShapeDtypeStruct((n, D_MODEL), F32),
        compiler_params=pltpu.CompilerParams(dimension_semantics=("arbitrary",), vmem_limit_bytes=VMEM_LIMIT),
        name="mlp",
    )(x, g.reshape(1, -1), w1_bf, w2_bf, g_final.reshape(1, -1))


def kernel(x_prompt, x_sample, cache_k, cache_v, page_table, g_mix, w_in, b_gate, b_sb, g_sv, w_sg, b_sg,
           w_o_att, w_o_sg, w_o, g_ffn, w_ff1, w_ff2, g_final):
    batch, seq, _ = x_prompt.shape
    dec_batch, dec_seq, _ = x_sample.shape
    depth = w_in.shape[0]
    n_pool = cache_k.shape[1]
    xp = x_prompt.reshape(batch * seq, D_MODEL)
    xs = x_sample.reshape(dec_batch * dec_seq, D_MODEL)
    ck = cache_k.transpose(0, 1, 3, 4, 2)
    cv = cache_v.transpose(0, 1, 3, 4, 2)
    kv_prompt, kv_sample, sgv_l = None, None, []
    for l in range(depth):
        w_in_bf = w_in[l].astype(BF16)
        woa_bf, wos_bf, wo_bf = w_o_att[l].astype(BF16), w_o_sg[l].astype(BF16), w_o[l].astype(BF16)
        w1_bf, w2_bf = w_ff1[l].astype(BF16), w_ff2[l].astype(BF16)
        bias2 = -b_sb[l] * LOG2E
        final = l == depth - 1

        q, k_all, v_all, kb, vb, u, sv, gates = _in_projection(
            xp, g_mix[l], w_in_bf, b_gate[l], g_sv[l], l, depth, kv_prompt)
        kv_prompt = (k_all, v_all)
        att = _attention_prompt(q.reshape(batch, seq, ATT_WIDTH), kb.reshape(batch, seq, ATT_WIDTH),
                                vb.reshape(batch, seq, ATT_WIDTH), bias2).reshape(batch * seq, ATT_WIDTH)
        xp = _merge(att, u, sv, gates, xp, woa_bf, wos_bf, wo_bf, w_sg[l], b_sg[l], CHUNK)

        q, k_all, v_all, _, _, u, sv, gates = _in_projection(
            xs, g_mix[l], w_in_bf, b_gate[l], g_sv[l], l, depth, kv_sample)
        kv_sample = (k_all, v_all)
        shp = (dec_batch, dec_seq, ATT_WIDTH)
        att = _attention_sample(q.reshape(shp), k_all[l].reshape(shp), v_all[l].reshape(shp), ck, cv, l,
                                page_table, bias2)
        xs = _merge(att, u, sv, gates, xs, woa_bf, wos_bf, wo_bf, w_sg[l], b_sg[l], dec_seq)
        sgv_l.append(sv.reshape(dec_batch, dec_seq, SG_GROUPS, SG_GROUP_DIM))

        xp = _mlp(xp, g_ffn[l], w1_bf, w2_bf, g_final, final)
        xs = _mlp(xs, g_ffn[l], w1_bf, w2_bf, g_final, final)

    prompt_heads = (depth, batch, seq, N_HEADS, HEAD_DIM)
    sample_heads = (depth, dec_batch, dec_seq, N_HEADS, HEAD_DIM)
    return (xp.reshape(batch, seq, D_MODEL), xs.reshape(dec_batch, dec_seq, D_MODEL),
            kv_prompt[0].reshape(prompt_heads), kv_prompt[1].reshape(prompt_heads),
            kv_sample[0].reshape(sample_heads), kv_sample[1].reshape(sample_heads), jnp.stack(sgv_l))
```

```python
import functools
import math

import jax
import jax.numpy as jnp
from jax import lax
from jax.experimental import pallas as pl
from jax.experimental.pallas import tpu as pltpu

F32 = jnp.float32
BF16 = jnp.bfloat16

D_MODEL = 1024
N_HEADS = 8
HEAD_DIM = 64
ATT_WIDTH = N_HEADS * HEAD_DIM
SG_GROUPS = 8
SG_GROUP_DIM = 64
SG_WIDTH = SG_GROUPS * SG_GROUP_DIM
CHUNK = 128
D_FF = 4 * D_MODEL
PAGE_SIZE = 128
EPS = 1e-6
IN_WIDTH = 3 * ATT_WIDTH + 2 * SG_WIDTH + 2 * D_MODEL

LANES = 128
SUBLANES = 8
LOG2E = math.log2(math.e)
Q_SCALE = -(HEAD_DIM ** -0.5) * LOG2E

TOKEN_TILE = 512
ATT_BQ = 1024
ATT_BK = 256
FF_CHUNK = 1024
BIAS_PIECES = 3
VMEM_LIMIT = 56 * 1024 * 1024


def _rms(x, g):
    return x * lax.rsqrt(jnp.mean(x * x, axis=-1, keepdims=True) + EPS) * g


def _const_spec(shape):
    return pl.BlockSpec(shape, lambda *_: (0,) * len(shape), pipeline_mode=pl.Buffered(1))


def _log2_one_minus_beta(nz):
    return jnp.minimum(nz, 0.0) - jnp.log(1.0 + jnp.exp2(-jnp.abs(nz))) * LOG2E


def _inproj_kernel(x_ref, g_ref, w_ref, bg_ref, gsv_ref, *rest):
    q_ref, k_ref, v_ref, kb_ref, vb_ref, u_ref, sv_ref, gate_ref = rest[-8:]
    xn = _rms(x_ref[...], g_ref[...]).astype(BF16)

    def proj(lo, hi):
        return jnp.dot(xn, w_ref[:, lo:hi], preferred_element_type=F32)

    o_k, o_v, o_u, o_sv, o_g = ATT_WIDTH, 2 * ATT_WIDTH, 3 * ATT_WIDTH, 3 * ATT_WIDTH + SG_WIDTH, 3 * ATT_WIDTH + 2 * SG_WIDTH
    q_ref[...] = (proj(0, o_k) * Q_SCALE).astype(BF16)
    def store_by_head(ref, val):
        tokens = val.shape[0]
        flat = ref.reshape(tokens * N_HEADS, HEAD_DIM)
        for h in range(N_HEADS):
            flat[pl.ds(h, tokens, stride=N_HEADS), :] = val[:, h * HEAD_DIM:(h + 1) * HEAD_DIM]

    k = proj(o_k, o_v)
    store_by_head(k_ref, k)
    kb_ref[...] = k.astype(BF16)
    v = proj(o_v, o_u)
    store_by_head(v_ref, v)
    vb_ref[...] = v.astype(BF16)
    u_ref[...] = jax.nn.gelu(proj(o_u, o_sv)).astype(u_ref.dtype)
    sv_ref[...] = _rms(jax.nn.gelu(proj(o_sv, o_g)), gsv_ref[...])
    gate_ref[...] = jax.nn.sigmoid(proj(o_g, IN_WIDTH) + bg_ref[...]).astype(gate_ref.dtype)


def _in_projection(x, g, w_bf, b_gate, g_sv, layer=0, depth=1, kv_all=None):
    n = x.shape[0]
    tm = min(TOKEN_TILE, n)
    row = lambda w: pl.BlockSpec((tm, w), lambda i: (i, 0))
    slab = pl.BlockSpec((None, tm, N_HEADS, HEAD_DIM), lambda i: (layer, i, 0, 0))
    in_specs = [row(D_MODEL), _const_spec((1, D_MODEL)), _const_spec((D_MODEL, IN_WIDTH)),
                _const_spec((1, 2 * D_MODEL)), _const_spec((1, SG_WIDTH))]
    args = [x, g.reshape(1, -1), w_bf, b_gate.reshape(1, -1), g_sv.reshape(1, -1)]
    aliases = {}
    if kv_all is not None:
        aliases = {len(args): 1, len(args) + 1: 2}
        in_specs += [pl.BlockSpec(memory_space=pl.ANY)] * 2
        args += list(kv_all)
    return pl.pallas_call(
        _inproj_kernel,
        grid=(n // tm,),
        in_specs=in_specs,
        out_specs=[row(ATT_WIDTH), slab, slab] + [row(ATT_WIDTH)] * 2 + [row(SG_WIDTH)] * 2 + [row(2 * D_MODEL)],
        out_shape=[jax.ShapeDtypeStruct((n, ATT_WIDTH), BF16),
                   jax.ShapeDtypeStruct((depth, n, N_HEADS, HEAD_DIM), F32),
                   jax.ShapeDtypeStruct((depth, n, N_HEADS, HEAD_DIM), F32),
                   jax.ShapeDtypeStruct((n, ATT_WIDTH), BF16),
                   jax.ShapeDtypeStruct((n, ATT_WIDTH), BF16),
                   jax.ShapeDtypeStruct((n, SG_WIDTH), BF16),
                   jax.ShapeDtypeStruct((n, SG_WIDTH), F32),
                   jax.ShapeDtypeStruct((n, 2 * D_MODEL), BF16)],
        input_output_aliases=aliases,
        compiler_params=pltpu.CompilerParams(dimension_semantics=("arbitrary",), vmem_limit_bytes=VMEM_LIMIT),
        name="in_projection",
    )(*args)


def _attn_prompt_kernel(q_ref, k_ref, v_ref, bias_ref, o_ref, kst, vst, u2, o_acc, c_acc,
                        nz_even, nz_odd, a_even, a_odd, *, bq, bk, seq):
    i = pl.program_id(2)
    nkb = seq // bk
    per_tile = bq // bk
    reps = bk // LANES
    nz_bufs = (nz_even, nz_odd)
    a_bufs = (a_even, a_odd)

    @pl.when(i == 0)
    def _():
        lane = lax.broadcasted_iota(jnp.int32, (1, LANES), 1)
        even = jnp.where(lane < HEAD_DIM, 1.0, 0.0).astype(BF16)
        odd = jnp.where(lane < HEAD_DIM, 0.0, 1.0).astype(BF16)
        k3 = k_ref[...].reshape(nkb, bk, LANES)
        kst[:, 0:bk, 0:LANES] = k3 * even
        kst[:, bk:2 * bk, 0:LANES] = k3 * odd
        kst[:, 0:bk, LANES:] = jnp.broadcast_to(bias_ref[0:1, :][None], (nkb, bk, LANES)).astype(BF16)
        kst[:, bk:2 * bk, LANES:] = jnp.broadcast_to(bias_ref[1:2, :][None], (nkb, bk, LANES)).astype(BF16)
        v3 = v_ref[...].reshape(nkb, bk, LANES)
        vst[:, 0:bk, :] = v3 * even
        vst[:, bk:2 * bk, :] = v3 * odd
        r = lax.broadcasted_iota(jnp.int32, (bk, bk), 0)
        c = lax.broadcasted_iota(jnp.int32, (bk, bk), 1)
        u2[...] = jnp.where(r > c, 1.0, 0.0).astype(BF16)

    ones = jnp.where(lax.broadcasted_iota(jnp.int32, (bq, LANES), 1) < BIAS_PIECES, 1.0, 0.0).astype(BF16)
    q_ext = jnp.concatenate([q_ref[...], ones], axis=1)
    o_acc[...] = jnp.zeros_like(o_acc)
    c_acc[...] = jnp.zeros_like(c_acc)

    def scores(j, r0=0):
        return lax.dot_general(q_ext[r0:], kst[j], (((1,), (1,)), ((), ())), preferred_element_type=F32)

    def weights(nz, mask, r0=0):
        n = bq - r0
        lsn = _log2_one_minus_beta(nz)
        if mask is not None:
            lsn = jnp.where(mask, lsn, 0.0)
        log2_beta = lsn - nz
        stacked = jnp.concatenate([lsn[:, :bk], lsn[:, bk:]], axis=0).astype(BF16)
        suf = jnp.dot(stacked, u2[...], preferred_element_type=F32)
        suffix = jnp.concatenate([suf[:n], suf[n:]], axis=1)
        c = c_acc[r0:, :]
        carry = jnp.concatenate([c[:, :LANES]] * reps + [c[:, LANES:]] * reps, axis=1)
        a = jnp.exp2(log2_beta + suffix + carry)
        if mask is not None:
            a = jnp.where(mask, a, 0.0)
        t_even = jnp.sum(lsn[:, :bk], axis=1, keepdims=True)
        t_odd = jnp.sum(lsn[:, bk:], axis=1, keepdims=True)
        c_acc[r0:, :] = c + jnp.concatenate(
            [jnp.broadcast_to(t_even, (n, LANES)), jnp.broadcast_to(t_odd, (n, LANES))], axis=1)
        return a.astype(BF16)

    def accumulate(a, j, r0=0):
        o_acc[r0:, :] += jnp.dot(a, vst[j], preferred_element_type=F32)

    n_full = i * per_tile
    nz = scores(n_full + per_tile - 1, (per_tile - 1) * bk)
    pending = None
    for d in reversed(range(per_tile)):
        r0 = d * bk
        if d > 0:
            nz_next = scores(n_full + d - 1, r0 - bk)
        else:
            nz_next = scores(jnp.maximum(n_full - 1, 0))
        if pending is not None:
            accumulate(*pending)
        row = lax.broadcasted_iota(jnp.int32, (bq - r0, 2 * bk), 0)
        col = lax.broadcasted_iota(jnp.int32, (bq - r0, 2 * bk), 1) & (bk - 1)
        pending = (weights(nz, col < row, r0), n_full + d, r0)
        nz = nz_next

    a_bufs[0][...] = pending[0]
    nz_bufs[0][...] = nz

    def run_blocks(first, count):
        assert count % 2 == 0
        for step in range(count):
            j = first - step
            cur, nxt = step % 2, 1 - step % 2
            nz = nz_bufs[cur][...]
            nz_bufs[nxt][...] = scores(jnp.maximum(j - 1, 0))
            accumulate(a_bufs[cur][...], j + 1)
            a_bufs[nxt][...] = weights(nz, None)

    def body(t, _):
        run_blocks(n_full - 1 - 2 * per_tile * t, 2 * per_tile)
        return 0

    lax.fori_loop(0, lax.shift_right_logical(i, 1), body, 0)

    @pl.when((i & 1) == 1)
    def _():
        run_blocks(per_tile - 1, per_tile)

    accumulate(a_bufs[0][...], 0)
    o_ref[...] = o_acc[...].astype(o_ref.dtype)


def _bf16_pieces(x):
    pieces = []
    for _ in range(BIAS_PIECES):
        p = x.astype(BF16).astype(F32)
        pieces.append(p)
        x = x - p
    return jnp.stack(pieces, axis=-1)


def _attention_prompt(q, kb, vb, bias2):
    b, s, _ = q.shape
    bq, bk = ATT_BQ, ATT_BK
    n_pairs = N_HEADS // 2
    offs = jnp.pad(_bf16_pieces(bias2), ((0, 0), (0, LANES - BIAS_PIECES))).reshape(n_pairs, 2, LANES)
    kern = functools.partial(_attn_prompt_kernel, bq=bq, bk=bk, seq=s)
    return pl.pallas_call(
        kern,
        grid=(b, n_pairs, s // bq),
        in_specs=[pl.BlockSpec((None, bq, LANES), lambda b_, p, i: (b_, i, p)),
                  pl.BlockSpec((None, s, LANES), lambda b_, p, i: (b_, 0, p)),
                  pl.BlockSpec((None, s, LANES), lambda b_, p, i: (b_, 0, p)),
                  pl.BlockSpec((None, 2, LANES), lambda b_, p, i: (p, 0, 0))],
        out_specs=pl.BlockSpec((None, bq, LANES), lambda b_, p, i: (b_, i, p)),
        out_shape=jax.ShapeDtypeStruct((b, s, ATT_WIDTH), BF16),
        scratch_shapes=[pltpu.VMEM((s // bk, 2 * bk, 2 * LANES), BF16),
                        pltpu.VMEM((s // bk, 2 * bk, LANES), BF16),
                        pltpu.VMEM((bk, bk), BF16),
                        pltpu.VMEM((bq, LANES), F32),
                        pltpu.VMEM((bq, 2 * LANES), F32),
                        pltpu.VMEM((bq, 2 * bk), F32),
                        pltpu.VMEM((bq, 2 * bk), F32),
                        pltpu.VMEM((bq, 2 * bk), BF16),
                        pltpu.VMEM((bq, 2 * bk), BF16)],
        compiler_params=pltpu.CompilerParams(dimension_semantics=("arbitrary",) * 3, vmem_limit_bytes=VMEM_LIMIT),
        name="attention_prompt",
    )(q, kb, vb, offs)


def _attn_sample_kernel(pt_ref, q_ref, kn_ref, vn_ref, bias_ref, *rest, n_pages):
    del pt_ref
    k_refs = rest[:n_pages]
    v_refs = rest[n_pages:2 * n_pages]
    o_ref = rest[2 * n_pages]
    kn_scr, vn_scr, u_scr = rest[2 * n_pages + 1:]
    dec_seq = q_ref.shape[0]
    rows = dec_seq * N_HEADS

    @pl.when(pl.program_id(0) == 0)
    def _():
        kn_scr[...] = jnp.zeros_like(kn_scr)
        vn_scr[...] = jnp.zeros_like(vn_scr)
        r = lax.broadcasted_iota(jnp.int32, (PAGE_SIZE, PAGE_SIZE), 0)
        c = lax.broadcasted_iota(jnp.int32, (PAGE_SIZE, PAGE_SIZE), 1)
        u_scr[...] = jnp.where(r > c, 1.0, 0.0).astype(BF16)

    kn_scr[0:dec_seq, :] = kn_ref[...]
    vn_scr[0:dec_seq, :] = vn_ref[...]
    bias = bias_ref[...]
    head_of_lane = lax.broadcasted_iota(jnp.int32, (N_HEADS, ATT_WIDTH), 1) >> (HEAD_DIM.bit_length() - 1)
    own_lanes = head_of_lane == lax.broadcasted_iota(jnp.int32, (N_HEADS, ATT_WIDTH), 0)
    qf = q_ref[...].astype(F32)
    q_rows = jnp.concatenate(
        [jnp.where(own_lanes, jnp.broadcast_to(qf[t:t + 1, :], (N_HEADS, ATT_WIDTH)), 0.0) for t in range(dec_seq)],
        axis=0).astype(BF16)

    pages = list(reversed(range(n_pages)))
    contract_last = (((1,), (1,)), ((), ()))
    t_new = lax.broadcasted_iota(jnp.int32, (rows, PAGE_SIZE), 0) >> (N_HEADS.bit_length() - 1)
    new_mask = lax.broadcasted_iota(jnp.int32, (rows, PAGE_SIZE), 1) < t_new

    def page_matrix(ref):
        return ref.reshape(ATT_WIDTH, PAGE_SIZE)[...].astype(BF16)

    nzs = [lax.dot_general(q_rows, kn_scr[...].astype(BF16), contract_last, preferred_element_type=F32) + bias]
    nzs += [jnp.dot(q_rows, page_matrix(k_refs[p]), preferred_element_type=F32) + bias for p in pages]
    lsns = [_log2_one_minus_beta(nz) for nz in nzs]
    lsns[0] = jnp.where(new_mask, lsns[0], 0.0)
    suffix = jnp.dot(jnp.concatenate(lsns, axis=0).astype(BF16), u_scr[...], preferred_element_type=F32)
    carry = jnp.zeros((rows, 1), F32)
    acc = jnp.zeros((rows, ATT_WIDTH), F32)
    for b, (nz, lsn) in enumerate(zip(nzs, lsns)):
        a = jnp.exp2(lsn - nz + suffix[b * rows:(b + 1) * rows] + carry)
        if b == 0:
            a = jnp.where(new_mask, a, 0.0).astype(BF16)
            acc = acc + jnp.dot(a, vn_scr[...].astype(BF16), preferred_element_type=F32)
        else:
            acc = acc + lax.dot_general(a.astype(BF16), page_matrix(v_refs[pages[b - 1]]), contract_last,
                                        preferred_element_type=F32)
        carry = carry + jnp.sum(lsn, axis=1, keepdims=True)
    own = jnp.where(jnp.concatenate([own_lanes] * dec_seq, axis=0), acc, 0.0)
    o_ref[...] = jnp.sum(own.reshape(dec_seq, N_HEADS, ATT_WIDTH), axis=1)


def _attention_sample(q, k_new, v_new, cache_k, cache_v, layer, page_table, bias2):
    db, t, _ = q.shape
    n_pages = page_table.shape[1]
    rows = t * N_HEADS
    bias_rows = jnp.broadcast_to(jnp.tile(bias2, t)[:, None], (rows, PAGE_SIZE))
    new_tokens = lambda: pl.BlockSpec((None, t, ATT_WIDTH), lambda b, pt: (b, 0, 0))
    page = lambda i: pl.BlockSpec((None, None, N_HEADS, HEAD_DIM, PAGE_SIZE),
                                  lambda b, pt, i=i: (layer, pt[b, i], 0, 0, 0))
    grid_spec = pltpu.PrefetchScalarGridSpec(
        num_scalar_prefetch=1,
        grid=(db,),
        in_specs=[new_tokens(), new_tokens(), new_tokens(), pl.BlockSpec((rows, PAGE_SIZE), lambda b, pt: (0, 0))]
                 + [page(i) for i in range(n_pages)] * 2,
        out_specs=new_tokens(),
        scratch_shapes=[pltpu.VMEM((PAGE_SIZE, ATT_WIDTH), F32),
                        pltpu.VMEM((PAGE_SIZE, ATT_WIDTH), F32),
                        pltpu.VMEM((PAGE_SIZE, PAGE_SIZE), BF16)],
    )
    out = pl.pallas_call(
        functools.partial(_attn_sample_kernel, n_pages=n_pages),
        grid_spec=grid_spec,
        out_shape=jax.ShapeDtypeStruct((db, t, ATT_WIDTH), F32),
        compiler_params=pltpu.CompilerParams(dimension_semantics=("arbitrary",), vmem_limit_bytes=VMEM_LIMIT),
        name="attention_sample",
    )(page_table, q, k_new, v_new, bias_rows, *([cache_k] * n_pages), *([cache_v] * n_pages))
    return out.reshape(db * t, ATT_WIDTH).astype(BF16)


def _merge_kernel(att_ref, u_ref, sv_ref, gate_ref, x_ref, woa_ref, wos_ref, wo_ref, wmix_ref, bmix_ref,
                  o_ref, mix_scr, *, chunk_len, tm):
    @pl.when(pl.program_id(0) == 0)
    def _():
        shift = chunk_len.bit_length() - 1
        r = lax.broadcasted_iota(jnp.int32, (CHUNK, CHUNK), 0)
        c = lax.broadcasted_iota(jnp.int32, (CHUNK, CHUNK), 1)
        keep = jnp.logical_and((r >> shift) == (c >> shift), c <= r)
        mix_scr[...] = jnp.where(keep[None], wmix_ref[...], 0.0).astype(BF16)

    first_group = lax.broadcasted_iota(jnp.int32, (1, LANES), 1) < SG_GROUP_DIM
    pieces = []
    for ch in range(tm // CHUNK):
        rows = slice(ch * CHUNK, (ch + 1) * CHUNK)
        svc = sv_ref[rows, :].astype(BF16)
        cols = []
        for p in range(SG_WIDTH // LANES):
            blk = svc[:, p * LANES:(p + 1) * LANES]
            m0 = jnp.dot(mix_scr[2 * p], blk, preferred_element_type=F32)
            m1 = jnp.dot(mix_scr[2 * p + 1], blk, preferred_element_type=F32)
            cols.append(jnp.where(first_group, m0, m1))
        mixed = jnp.concatenate(cols, axis=1) + bmix_ref[...]
        pieces.append(u_ref[rows, :].astype(F32) * mixed)
    sg = jnp.concatenate(pieces, axis=0).astype(BF16)
    merged = (gate_ref[:, :D_MODEL].astype(F32) * jnp.dot(att_ref[...], woa_ref[...], preferred_element_type=F32)
              + gate_ref[:, D_MODEL:].astype(F32) * jnp.dot(sg, wos_ref[...], preferred_element_type=F32))
    o_ref[...] = x_ref[...] + jnp.dot(merged.astype(BF16), wo_ref[...], preferred_element_type=F32)


def _merge(att, u, sv, gates, x, woa_bf, wos_bf, wo_bf, w_sg, b_sg, chunk_len):
    n = x.shape[0]
    tm = min(TOKEN_TILE, n)
    reps = CHUNK // chunk_len
    wmix = jnp.tile(w_sg[:, :chunk_len, :chunk_len], (1, reps, reps))
    bmix = jnp.repeat(jnp.tile(b_sg[:, :chunk_len].T, (reps, 1)), SG_GROUP_DIM, axis=1)
    row = lambda w: pl.BlockSpec((tm, w), lambda i: (i, 0))
    return pl.pallas_call(
        functools.partial(_merge_kernel, chunk_len=chunk_len, tm=tm),
        grid=(n // tm,),
        in_specs=[row(ATT_WIDTH), row(SG_WIDTH), row(SG_WIDTH), row(2 * D_MODEL), row(D_MODEL),
                  _const_spec((ATT_WIDTH, D_MODEL)), _const_spec((SG_WIDTH, D_MODEL)),
                  _const_spec((D_MODEL, D_MODEL)), _const_spec((SG_GROUPS, CHUNK, CHUNK)),
                  _const_spec((CHUNK, SG_WIDTH))],
        out_specs=row(D_MODEL),
        out_shape=jax.ShapeDtypeStruct((n, D_MODEL), F32),
        scratch_shapes=[pltpu.VMEM((SG_GROUPS, CHUNK, CHUNK), BF16)],
        compiler_params=pltpu.CompilerParams(dimension_semantics=("arbitrary",), vmem_limit_bytes=VMEM_LIMIT),
        name="merge",
    )(att, u, sv, gates, x, woa_bf, wos_bf, wo_bf, wmix, bmix)


def _mlp_kernel(x_ref, g_ref, w1_ref, w2_ref, gf_ref, o_ref, *, final):
    x = x_ref[...]
    xn = _rms(x, g_ref[...]).astype(BF16)
    acc = x
    for c in range(D_FF // FF_CHUNK):
        cols = slice(c * FF_CHUNK, (c + 1) * FF_CHUNK)
        h = jnp.maximum(jnp.dot(xn, w1_ref[:, cols], preferred_element_type=F32), 0.0)
        acc = acc + jnp.dot((h * h).astype(BF16), w2_ref[cols, :], preferred_element_type=F32)
    if final:
        acc = _rms(acc, gf_ref[...])
    o_ref[...] = acc


def _mlp(x, g, w1_bf, w2_bf, g_final, final):
    n = x.shape[0]
    tm = min(TOKEN_TILE, n)
    row = pl.BlockSpec((tm, D_MODEL), lambda i: (i, 0))
    return pl.pallas_call(
        functools.partial(_mlp_kernel, final=final),
        grid=(n // tm,),
        in_specs=[row, _const_spec((1, D_MODEL)), _const_spec((D_MODEL, D_FF)), _const_spec((D_FF, D_MODEL)),
                  _const_spec((1, D_MODEL))],
        out_specs=row,
        out_shape=jax.ShapeDtypeStruct((n, D_MODEL), F32),
        compiler_params=pltpu.CompilerParams(dimension_semantics=("arbitrary",), vmem_limit_bytes=VMEM_LIMIT),
        name="mlp",
    )(x, g.reshape(1, -1), w1_bf, w2_bf, g_final.reshape(1, -1))


def kernel(x_prompt, x_sample, cache_k, cache_v, page_table, g_mix, w_in, b_gate, b_sb, g_sv, w_sg, b_sg,
           w_o_att, w_o_sg, w_o, g_ffn, w_ff1, w_ff2, g_final):
    batch, seq, _ = x_prompt.shape
    dec_batch, dec_seq, _ = x_sample.shape
    depth = w_in.shape[0]
    n_pool = cache_k.shape[1]
    xp = x_prompt.reshape(batch * seq, D_MODEL)
    xs = x_sample.reshape(dec_batch * dec_seq, D_MODEL)
    ck = cache_k.transpose(0, 1, 3, 4, 2)
    cv = cache_v.transpose(0, 1, 3, 4, 2)
    kv_prompt, kv_sample, sgv_l = None, None, []
    for l in range(depth):
        w_in_bf = w_in[l].astype(BF16)
        woa_bf, wos_bf, wo_bf = w_o_att[l].astype(BF16), w_o_sg[l].astype(BF16), w_o[l].astype(BF16)
        w1_bf, w2_bf = w_ff1[l].astype(BF16), w_ff2[l].astype(BF16)
        bias2 = -b_sb[l] * LOG2E
        final = l == depth - 1

        q, k_all, v_all, kb, vb, u, sv, gates = _in_projection(
            xp, g_mix[l], w_in_bf, b_gate[l], g_sv[l], l, depth, kv_prompt)
        kv_prompt = (k_all, v_all)
        att = _attention_prompt(q.reshape(batch, seq, ATT_WIDTH), kb.reshape(batch, seq, ATT_WIDTH),
                                vb.reshape(batch, seq, ATT_WIDTH), bias2).reshape(batch * seq, ATT_WIDTH)
        xp = _merge(att, u, sv, gates, xp, woa_bf, wos_bf, wo_bf, w_sg[l], b_sg[l], CHUNK)

        q, k_all, v_all, _, _, u, sv, gates = _in_projection(
            xs, g_mix[l], w_in_bf, b_gate[l], g_sv[l], l, depth, kv_sample)
        kv_sample = (k_all, v_all)
        shp = (dec_batch, dec_seq, ATT_WIDTH)
        att = _attention_sample(q.reshape(shp), k_all[l].reshape(shp), v_all[l].reshape(shp), ck, cv, l,
                                page_table, bias2)
        xs = _merge(att, u, sv, gates, xs, woa_bf, wos_bf, wo_bf, w_sg[l], b_sg[l], dec_seq)
        sgv_l.append(sv.reshape(dec_batch, dec_seq, SG_GROUPS, SG_GROUP_DIM))

        xp = _mlp(xp, g_ffn[l], w1_bf, w2_bf, g_final, final)
        xs = _mlp(xs, g_ffn[l], w1_bf, w2_bf, g_final, final)

    prompt_heads = (depth, batch, seq, N_HEADS, HEAD_DIM)
    sample_heads = (depth, dec_batch, dec_seq, N_HEADS, HEAD_DIM)
    return (xp.reshape(batch, seq, D_MODEL), xs.reshape(dec_batch, dec_seq, D_MODEL),
            kv_prompt[0].reshape(prompt_heads), kv_prompt[1].reshape(prompt_heads),
            kv_sample[0].reshape(sample_heads), kv_sample[1].reshape(sample_heads), jnp.stack(sgv_l))
```

```python
import functools
import math

import jax
import jax.numpy as jnp
from jax import lax
from jax.experimental import pallas as pl
from jax.experimental.pallas import tpu as pltpu

F32 = jnp.float32
BF16 = jnp.bfloat16

D_MODEL = 1024
N_HEADS = 8
HEAD_DIM = 64
ATT_WIDTH = N_HEADS * HEAD_DIM
SG_GROUPS = 8
SG_GROUP_DIM = 64
SG_WIDTH = SG_GROUPS * SG_GROUP_DIM
CHUNK = 128
D_FF = 4 * D_MODEL
PAGE_SIZE = 128
EPS = 1e-6
IN_WIDTH = 3 * ATT_WIDTH + 2 * SG_WIDTH + 2 * D_MODEL

LANES = 128
SUBLANES = 8
LOG2E = math.log2(math.e)
Q_SCALE = -(HEAD_DIM ** -0.5) * LOG2E

TOKEN_TILE = 512
ATT_BQ = 1024
ATT_BK = 256
FF_CHUNK = 1024
BIAS_PIECES = 3
VMEM_LIMIT = 56 * 1024 * 1024


def _rms(x, g):
    return x * lax.rsqrt(jnp.mean(x * x, axis=-1, keepdims=True) + EPS) * g


def _const_spec(shape):
    return pl.BlockSpec(shape, lambda *_: (0,) * len(shape), pipeline_mode=pl.Buffered(1))


def _log2_one_minus_beta(nz):
    return jnp.minimum(nz, 0.0) - jnp.log(1.0 + jnp.exp2(-jnp.abs(nz))) * LOG2E


def _inproj_kernel(x_ref, g_ref, w_ref, bg_ref, gsv_ref, *rest):
    q_ref, k_ref, v_ref, kb_ref, vb_ref, u_ref, sv_ref, gate_ref = rest[-8:]
    xn = _rms(x_ref[...], g_ref[...]).astype(BF16)

    def proj(lo, hi):
        return jnp.dot(xn, w_ref[:, lo:hi], preferred_element_type=F32)

    o_k, o_v, o_u, o_sv, o_g = ATT_WIDTH, 2 * ATT_WIDTH, 3 * ATT_WIDTH, 3 * ATT_WIDTH + SG_WIDTH, 3 * ATT_WIDTH + 2 * SG_WIDTH
    q_ref[...] = (proj(0, o_k) * Q_SCALE).astype(BF16)
    def store_by_head(ref, val):
        tokens = val.shape[0]
        flat = ref.reshape(tokens * N_HEADS, HEAD_DIM)
        for h in range(N_HEADS):
            flat[pl.ds(h, tokens, stride=N_HEADS), :] = val[:, h * HEAD_DIM:(h + 1) * HEAD_DIM]

    k = proj(o_k, o_v)
    store_by_head(k_ref, k)
    kb_ref[...] = k.astype(BF16)
    v = proj(o_v, o_u)
    store_by_head(v_ref, v)
    vb_ref[...] = v.astype(BF16)
    u_ref[...] = jax.nn.gelu(proj(o_u, o_sv)).astype(u_ref.dtype)
    sv_ref[...] = _rms(jax.nn.gelu(proj(o_sv, o_g)), gsv_ref[...])
    gate_ref[...] = jax.nn.sigmoid(proj(o_g, IN_WIDTH) + bg_ref[...]).astype(gate_ref.dtype)


def _in_projection(x, g, w_bf, b_gate, g_sv, layer=0, depth=1, kv_all=None):
    n = x.shape[0]
    tm = min(TOKEN_TILE, n)
    row = lambda w: pl.BlockSpec((tm, w), lambda i: (i, 0))
    slab = pl.BlockSpec((None, tm, N_HEADS, HEAD_DIM), lambda i: (layer, i, 0, 0))
    in_specs = [row(D_MODEL), _const_spec((1, D_MODEL)), _const_spec((D_MODEL, IN_WIDTH)),
                _const_spec((1, 2 * D_MODEL)), _const_spec((1, SG_WIDTH))]
    args = [x, g.reshape(1, -1), w_bf, b_gate.reshape(1, -1), g_sv.reshape(1, -1)]
    aliases = {}
    if kv_all is not None:
        aliases = {len(args): 1, len(args) + 1: 2}
        in_specs += [pl.BlockSpec(memory_space=pl.ANY)] * 2
        args += list(kv_all)
    return pl.pallas_call(
        _inproj_kernel,
        grid=(n // tm,),
        in_specs=in_specs,
        out_specs=[row(ATT_WIDTH), slab, slab] + [row(ATT_WIDTH)] * 2 + [row(SG_WIDTH)] * 2 + [row(2 * D_MODEL)],
        out_shape=[jax.ShapeDtypeStruct((n, ATT_WIDTH), BF16),
                   jax.ShapeDtypeStruct((depth, n, N_HEADS, HEAD_DIM), F32),
                   jax.ShapeDtypeStruct((depth, n, N_HEADS, HEAD_DIM), F32),
                   jax.ShapeDtypeStruct((n, ATT_WIDTH), BF16),
                   jax.ShapeDtypeStruct((n, ATT_WIDTH), BF16),
                   jax.ShapeDtypeStruct((n, SG_WIDTH), BF16),
                   jax.ShapeDtypeStruct((n, SG_WIDTH), F32),
                   jax.ShapeDtypeStruct((n, 2 * D_MODEL), BF16)],
        input_output_aliases=aliases,
        compiler_params=pltpu.CompilerParams(dimension_semantics=("arbitrary",), vmem_limit_bytes=VMEM_LIMIT),
        name="in_projection",
    )(*args)


def _attn_prompt_kernel(q_ref, k_ref, v_ref, bias_ref, o_ref, kst, vst, u2, o_acc, c_acc,
                        nz_even, nz_odd, a_even, a_odd, *, bq, bk, seq):
    i = pl.program_id(2)
    nkb = seq // bk
    per_tile = bq // bk
    reps = bk // LANES
    nz_bufs = (nz_even, nz_odd)
    a_bufs = (a_even, a_odd)

    @pl.when(i == 0)
    def _():
        lane = lax.broadcasted_iota(jnp.int32, (1, LANES), 1)
        even = jnp.where(lane < HEAD_DIM, 1.0, 0.0).astype(BF16)
        odd = jnp.where(lane < HEAD_DIM, 0.0, 1.0).astype(BF16)
        k3 = k_ref[...].reshape(nkb, bk, LANES)
        kst[:, 0:bk, 0:LANES] = k3 * even
        kst[:, bk:2 * bk, 0:LANES] = k3 * odd
        kst[:, 0:bk, LANES:] = jnp.broadcast_to(bias_ref[0:1, :][None], (nkb, bk, LANES)).astype(BF16)
        kst[:, bk:2 * bk, LANES:] = jnp.broadcast_to(bias_ref[1:2, :][None], (nkb, bk, LANES)).astype(BF16)
        v3 = v_ref[...].reshape(nkb, bk, LANES)
        vst[:, 0:bk, :] = v3 * even
        vst[:, bk:2 * bk, :] = v3 * odd
        r = lax.broadcasted_iota(jnp.int32, (bk, bk), 0)
        c = lax.broadcasted_iota(jnp.int32, (bk, bk), 1)
        u2[...] = jnp.where(r > c, 1.0, 0.0).astype(BF16)

    ones = jnp.where(lax.broadcasted_iota(jnp.int32, (bq, LANES), 1) < BIAS_PIECES, 1.0, 0.0).astype(BF16)
    q_ext = jnp.concatenate([q_ref[...], ones], axis=1)
    o_acc[...] = jnp.zeros_like(o_acc)
    c_acc[...] = jnp.zeros_like(c_acc)

    def scores(j, r0=0):
        return lax.dot_general(q_ext[r0:], kst[j], (((1,), (1,)), ((), ())), preferred_element_type=F32)

    def weights(nz, mask, r0=0):
        n = bq - r0
        lsn = _log2_one_minus_beta(nz)
        if mask is not None:
            lsn = jnp.where(mask, lsn, 0.0)
        log2_beta = lsn - nz
        stacked = jnp.concatenate([lsn[:, :bk], lsn[:, bk:]], axis=0).astype(BF16)
        suf = jnp.dot(stacked, u2[...], preferred_element_type=F32)
        suffix = jnp.concatenate([suf[:n], suf[n:]], axis=1)
        c = c_acc[r0:, :]
        carry = jnp.concatenate([c[:, :LANES]] * reps + [c[:, LANES:]] * reps, axis=1)
        a = jnp.exp2(log2_beta + suffix + carry)
        if mask is not None:
            a = jnp.where(mask, a, 0.0)
        t_even = jnp.sum(lsn[:, :bk], axis=1, keepdims=True)
        t_odd = jnp.sum(lsn[:, bk:], axis=1, keepdims=True)
        c_acc[r0:, :] = c + jnp.concatenate(
            [jnp.broadcast_to(t_even, (n, LANES)), jnp.broadcast_to(t_odd, (n, LANES))], axis=1)
        return a.astype(BF16)

    def accumulate(a, j, r0=0):
        o_acc[r0:, :] += jnp.dot(a, vst[j], preferred_element_type=F32)

    n_full = i * per_tile
    nz = scores(n_full + per_tile - 1, (per_tile - 1) * bk)
    pending = None
    for d in reversed(range(per_tile)):
        r0 = d * bk
        if d > 0:
            nz_next = scores(n_full + d - 1, r0 - bk)
        else:
            nz_next = scores(jnp.maximum(n_full - 1, 0))
        if pending is not None:
            accumulate(*pending)
        row = lax.broadcasted_iota(jnp.int32, (bq - r0, 2 * bk), 0)
        col = lax.broadcasted_iota(jnp.int32, (bq - r0, 2 * bk), 1) & (bk - 1)
        pending = (weights(nz, col < row, r0), n_full + d, r0)
        nz = nz_next

    a_bufs[0][...] = pending[0]
    nz_bufs[0][...] = nz

    def run_blocks(first, count):
        assert count % 2 == 0
        for step in range(count):
            j = first - step
            cur, nxt = step % 2, 1 - step % 2
            nz = nz_bufs[cur][...]
            nz_bufs[nxt][...] = scores(jnp.maximum(j - 1, 0))
            accumulate(a_bufs[cur][...], j + 1)
            a_bufs[nxt][...] = weights(nz, None)

    def body(t, _):
        run_blocks(n_full - 1 - 2 * per_tile * t, 2 * per_tile)
        return 0

    lax.fori_loop(0, lax.shift_right_logical(i, 1), body, 0)

    @pl.when((i & 1) == 1)
    def _():
        run_blocks(per_tile - 1, per_tile)

    accumulate(a_bufs[0][...], 0)
    o_ref[...] = o_acc[...].astype(o_ref.dtype)


def _bf16_pieces(x):
    pieces = []
    for _ in range(BIAS_PIECES):
        p = x.astype(BF16).astype(F32)
        pieces.append(p)
        x = x - p
    return jnp.stack(pieces, axis=-1)


def _attention_prompt(q, kb, vb, bias2):
    b, s, _ = q.shape
    bq, bk = ATT_BQ, ATT_BK
    n_pairs = N_HEADS // 2
    offs = jnp.pad(_bf16_pieces(bias2), ((0, 0), (0, LANES - BIAS_PIECES))).reshape(n_pairs, 2, LANES)
    kern = functools.partial(_attn_prompt_kernel, bq=bq, bk=bk, seq=s)
    return pl.pallas_call(
        kern,
        grid=(b, n_pairs, s // bq),
        in_specs=[pl.BlockSpec((None, bq, LANES), lambda b_, p, i: (b_, i, p)),
                  pl.BlockSpec((None, s, LANES), lambda b_, p, i: (b_, 0, p)),
                  pl.BlockSpec((None, s, LANES), lambda b_, p, i: (b_, 0, p)),
                  pl.BlockSpec((None, 2, LANES), lambda b_, p, i: (p, 0, 0))],
        out_specs=pl.BlockSpec((None, bq, LANES), lambda b_, p, i: (b_, i, p)),
        out_shape=jax.ShapeDtypeStruct((b, s, ATT_WIDTH), BF16),
        scratch_shapes=[pltpu.VMEM((s // bk, 2 * bk, 2 * LANES), BF16),
                        pltpu.VMEM((s // bk, 2 * bk, LANES), BF16),
                        pltpu.VMEM((bk, bk), BF16),
                        pltpu.VMEM((bq, LANES), F32),
                        pltpu.VMEM((bq, 2 * LANES), F32),
                        pltpu.VMEM((bq, 2 * bk), F32),
                        pltpu.VMEM((bq, 2 * bk), F32),
                        pltpu.VMEM((bq, 2 * bk), BF16),
                        pltpu.VMEM((bq, 2 * bk), BF16)],
        compiler_params=pltpu.CompilerParams(dimension_semantics=("arbitrary",) * 3, vmem_limit_bytes=VMEM_LIMIT),
        name="attention_prompt",
    )(q, kb, vb, offs)


def _attn_sample_kernel(pt_ref, q_ref, kn_ref, vn_ref, bias_ref, *rest, n_pages):
    del pt_ref
    k_refs = rest[:n_pages]
    v_refs = rest[n_pages:2 * n_pages]
    o_ref = rest[2 * n_pages]
    kn_scr, vn_scr, u_scr = rest[2 * n_pages + 1:]
    dec_seq = q_ref.shape[0]
    rows = dec_seq * N_HEADS

    @pl.when(pl.program_id(0) == 0)
    def _():
        kn_scr[...] = jnp.zeros_like(kn_scr)
        vn_scr[...] = jnp.zeros_like(vn_scr)
        r = lax.broadcasted_iota(jnp.int32, (PAGE_SIZE, PAGE_SIZE), 0)
        c = lax.broadcasted_iota(jnp.int32, (PAGE_SIZE, PAGE_SIZE), 1)
        u_scr[...] = jnp.where(r > c, 1.0, 0.0).astype(BF16)

    kn_scr[0:dec_seq, :] = kn_ref[...]
    vn_scr[0:dec_seq, :] = vn_ref[...]
    bias = bias_ref[...]
    head_of_lane = lax.broadcasted_iota(jnp.int32, (N_HEADS, ATT_WIDTH), 1) >> (HEAD_DIM.bit_length() - 1)
    own_lanes = head_of_lane == lax.broadcasted_iota(jnp.int32, (N_HEADS, ATT_WIDTH), 0)
    qf = q_ref[...].astype(F32)
    q_rows = jnp.concatenate(
        [jnp.where(own_lanes, jnp.broadcast_to(qf[t:t + 1, :], (N_HEADS, ATT_WIDTH)), 0.0) for t in range(dec_seq)],
        axis=0).astype(BF16)

    pages = list(reversed(range(n_pages)))
    contract_last = (((1,), (1,)), ((), ()))
    t_new = lax.broadcasted_iota(jnp.int32, (rows, PAGE_SIZE), 0) >> (N_HEADS.bit_length() - 1)
    new_mask = lax.broadcasted_iota(jnp.int32, (rows, PAGE_SIZE), 1) < t_new

    def page_matrix(ref):
        return ref.reshape(ATT_WIDTH, PAGE_SIZE)[...].astype(BF16)

    nzs = [lax.dot_general(q_rows, kn_scr[...].astype(BF16), contract_last, preferred_element_type=F32) + bias]
    nzs += [jnp.dot(q_rows, page_matrix(k_refs[p]), preferred_element_type=F32) + bias for p in pages]
    lsns = [_log2_one_minus_beta(nz) for nz in nzs]
    lsns[0] = jnp.where(new_mask, lsns[0], 0.0)
    suffix = jnp.dot(jnp.concatenate(lsns, axis=0).astype(BF16), u_scr[...], preferred_element_type=F32)
    carry = jnp.zeros((rows, 1), F32)
    acc = jnp.zeros((rows, ATT_WIDTH), F32)
    for b, (nz, lsn) in enumerate(zip(nzs, lsns)):
        a = jnp.exp2(lsn - nz + suffix[b * rows:(b + 1) * rows] + carry)
        if b == 0:
            a = jnp.where(new_mask, a, 0.0).astype(BF16)
            acc = acc + jnp.dot(a, vn_scr[...].astype(BF16), preferred_element_type=F32)
        else:
            acc = acc + lax.dot_general(a.astype(BF16), page_matrix(v_refs[pages[b - 1]]), contract_last,
                                        preferred_element_type=F32)
        carry = carry + jnp.sum(lsn, axis=1, keepdims=True)
    own = jnp.where(jnp.concatenate([own_lanes] * dec_seq, axis=0), acc, 0.0)
    o_ref[...] = jnp.sum(own.reshape(dec_seq, N_HEADS, ATT_WIDTH), axis=1)


def _attention_sample(q, k_new, v_new, cache_k, cache_v, layer, page_table, bias2):
    db, t, _ = q.shape
    n_pages = page_table.shape[1]
    rows = t * N_HEADS
    bias_rows = jnp.broadcast_to(jnp.tile(bias2, t)[:, None], (rows, PAGE_SIZE))
    new_tokens = lambda: pl.BlockSpec((None, t, ATT_WIDTH), lambda b, pt: (b, 0, 0))
    page = lambda i: pl.BlockSpec((None, None, N_HEADS, HEAD_DIM, PAGE_SIZE),
                                  lambda b, pt, i=i: (layer, pt[b, i], 0, 0, 0))
    grid_spec = pltpu.PrefetchScalarGridSpec(
        num_scalar_prefetch=1,
        grid=(db,),
        in_specs=[new_tokens(), new_tokens(), new_tokens(), pl.BlockSpec((rows, PAGE_SIZE), lambda b, pt: (0, 0))]
                 + [page(i) for i in range(n_pages)] * 2,
        out_specs=new_tokens(),
        scratch_shapes=[pltpu.VMEM((PAGE_SIZE, ATT_WIDTH), F32),
                        pltpu.VMEM((PAGE_SIZE, ATT_WIDTH), F32),
                        pltpu.VMEM((PAGE_SIZE, PAGE_SIZE), BF16)],
    )
    out = pl.pallas_call(
        functools.partial(_attn_sample_kernel, n_pages=n_pages),
        grid_spec=grid_spec,
        out_shape=jax.ShapeDtypeStruct((db, t, ATT_WIDTH), F32),
        compiler_params=pltpu.CompilerParams(dimension_semantics=("arbitrary",), vmem_limit_bytes=VMEM_LIMIT),
        name="attention_sample",
    )(page_table, q, k_new, v_new, bias_rows, *([cache_k] * n_pages), *([cache_v] * n_pages))
    return out.reshape(db * t, ATT_WIDTH).astype(BF16)


def _squared_relu_mlp(x, g_ref, w1_ref, w2_ref, gf_ref, final):
    xn = _rms(x, g_ref[...]).astype(BF16)
    acc = x
    for c in range(D_FF // FF_CHUNK):
        cols = slice(c * FF_CHUNK, (c + 1) * FF_CHUNK)
        h = jnp.maximum(jnp.dot(xn, w1_ref[:, cols], preferred_element_type=F32), 0.0)
        acc = acc + jnp.dot((h * h).astype(BF16), w2_ref[cols, :], preferred_element_type=F32)
    if final:
        acc = _rms(acc, gf_ref[...])
    return acc


def _merge_mlp_kernel(att_ref, u_ref, sv_ref, gate_ref, x_ref, woa_ref, wos_ref, wo_ref, wmix_ref, bmix_ref,
                      g_ref, w1_ref, w2_ref, gf_ref, o_ref, mix_scr, *, chunk_len, tm, final):
    @pl.when(pl.program_id(0) == 0)
    def _():
        shift = chunk_len.bit_length() - 1
        r = lax.broadcasted_iota(jnp.int32, (CHUNK, CHUNK), 0)
        c = lax.broadcasted_iota(jnp.int32, (CHUNK, CHUNK), 1)
        keep = jnp.logical_and((r >> shift) == (c >> shift), c <= r)
        mix_scr[...] = jnp.where(keep[None], wmix_ref[...], 0.0).astype(BF16)

    first_group = lax.broadcasted_iota(jnp.int32, (1, LANES), 1) < SG_GROUP_DIM
    pieces = []
    for ch in range(tm // CHUNK):
        rows = slice(ch * CHUNK, (ch + 1) * CHUNK)
        svc = sv_ref[rows, :].astype(BF16)
        cols = []
        for p in range(SG_WIDTH // LANES):
            blk = svc[:, p * LANES:(p + 1) * LANES]
            m0 = jnp.dot(mix_scr[2 * p], blk, preferred_element_type=F32)
            m1 = jnp.dot(mix_scr[2 * p + 1], blk, preferred_element_type=F32)
            cols.append(jnp.where(first_group, m0, m1))
        mixed = jnp.concatenate(cols, axis=1) + bmix_ref[...]
        pieces.append(u_ref[rows, :].astype(F32) * mixed)
    sg = jnp.concatenate(pieces, axis=0).astype(BF16)
    merged = (gate_ref[:, :D_MODEL].astype(F32) * jnp.dot(att_ref[...], woa_ref[...], preferred_element_type=F32)
              + gate_ref[:, D_MODEL:].astype(F32) * jnp.dot(sg, wos_ref[...], preferred_element_type=F32))
    x1 = x_ref[...] + jnp.dot(merged.astype(BF16), wo_ref[...], preferred_element_type=F32)
    o_ref[...] = _squared_relu_mlp(x1, g_ref, w1_ref, w2_ref, gf_ref, final)


def _merge_mlp(att, u, sv, gates, x, woa_bf, wos_bf, wo_bf, w_sg, b_sg, chunk_len, g_ffn, w1_bf, w2_bf, g_final,
               final):
    n = x.shape[0]
    tm = min(TOKEN_TILE, n)
    reps = CHUNK // chunk_len
    wmix = jnp.tile(w_sg[:, :chunk_len, :chunk_len], (1, reps, reps))
    bmix = jnp.repeat(jnp.tile(b_sg[:, :chunk_len].T, (reps, 1)), SG_GROUP_DIM, axis=1)
    row = lambda w: pl.BlockSpec((tm, w), lambda i: (i, 0))
    return pl.pallas_call(
        functools.partial(_merge_mlp_kernel, chunk_len=chunk_len, tm=tm, final=final),
        grid=(n // tm,),
        in_specs=[row(ATT_WIDTH), row(SG_WIDTH), row(SG_WIDTH), row(2 * D_MODEL), row(D_MODEL),
                  _const_spec((ATT_WIDTH, D_MODEL)), _const_spec((SG_WIDTH, D_MODEL)),
                  _const_spec((D_MODEL, D_MODEL)), _const_spec((SG_GROUPS, CHUNK, CHUNK)),
                  _const_spec((CHUNK, SG_WIDTH)),
                  _const_spec((1, D_MODEL)), _const_spec((D_MODEL, D_FF)), _const_spec((D_FF, D_MODEL)),
                  _const_spec((1, D_MODEL))],
        out_specs=row(D_MODEL),
        out_shape=jax.ShapeDtypeStruct((n, D_MODEL), F32),
        scratch_shapes=[pltpu.VMEM((SG_GROUPS, CHUNK, CHUNK), BF16)],
        compiler_params=pltpu.CompilerParams(dimension_semantics=("arbitrary",), vmem_limit_bytes=VMEM_LIMIT),
        name="merge_mlp",
    )(att, u, sv, gates, x, woa_bf, wos_bf, wo_bf, wmix, bmix,
      g_ffn.reshape(1, -1), w1_bf, w2_bf, g_final.reshape(1, -1))


def kernel(x_prompt, x_sample, cache_k, cache_v, page_table, g_mix, w_in, b_gate, b_sb, g_sv, w_sg, b_sg,
           w_o_att, w_o_sg, w_o, g_ffn, w_ff1, w_ff2, g_final):
    batch, seq, _ = x_prompt.shape
    dec_batch, dec_seq, _ = x_sample.shape
    depth = w_in.shape[0]
    n_pool = cache_k.shape[1]
    xp = x_prompt.reshape(batch * seq, D_MODEL)
    xs = x_sample.reshape(dec_batch * dec_seq, D_MODEL)
    ck = cache_k.transpose(0, 1, 3, 4, 2)
    cv = cache_v.transpose(0, 1, 3, 4, 2)
    kv_prompt, kv_sample, sgv_l = None, None, []
    for l in range(depth):
        w_in_bf = w_in[l].astype(BF16)
        woa_bf, wos_bf, wo_bf = w_o_att[l].astype(BF16), w_o_sg[l].astype(BF16), w_o[l].astype(BF16)
        w1_bf, w2_bf = w_ff1[l].astype(BF16), w_ff2[l].astype(BF16)
        bias2 = -b_sb[l] * LOG2E
        final = l == depth - 1

        q, k_all, v_all, kb, vb, u, sv, gates = _in_projection(
            xp, g_mix[l], w_in_bf, b_gate[l], g_sv[l], l, depth, kv_prompt)
        kv_prompt = (k_all, v_all)
        att = _attention_prompt(q.reshape(batch, seq, ATT_WIDTH), kb.reshape(batch, seq, ATT_WIDTH),
                                vb.reshape(batch, seq, ATT_WIDTH), bias2).reshape(batch * seq, ATT_WIDTH)
        xp = _merge_mlp(att, u, sv, gates, xp, woa_bf, wos_bf, wo_bf, w_sg[l], b_sg[l], CHUNK,
                        g_ffn[l], w1_bf, w2_bf, g_final, final)

        q, k_all, v_all, _, _, u, sv, gates = _in_projection(
            xs, g_mix[l], w_in_bf, b_gate[l], g_sv[l], l, depth, kv_sample)
        kv_sample = (k_all, v_all)
        shp = (dec_batch, dec_seq, ATT_WIDTH)
        att = _attention_sample(q.reshape(shp), k_all[l].reshape(shp), v_all[l].reshape(shp), ck, cv, l,
                                page_table, bias2)
        xs = _merge_mlp(att, u, sv, gates, xs, woa_bf, wos_bf, wo_bf, w_sg[l], b_sg[l], dec_seq,
                        g_ffn[l], w1_bf, w2_bf, g_final, final)
        sgv_l.append(sv.reshape(dec_batch, dec_seq, SG_GROUPS, SG_GROUP_DIM))

    prompt_heads = (depth, batch, seq, N_HEADS, HEAD_DIM)
    sample_heads = (depth, dec_batch, dec_seq, N_HEADS, HEAD_DIM)
    return (xp.reshape(batch, seq, D_MODEL), xs.reshape(dec_batch, dec_seq, D_MODEL),
            kv_prompt[0].reshape(prompt_heads), kv_prompt[1].reshape(prompt_heads),
            kv_sample[0].reshape(sample_heads), kv_sample[1].reshape(sample_heads), jnp.stack(sgv_l))
```
